```python
import jax
import jax.numpy as jnp
from jax import lax
import numpy as np

D_MODEL = 1024
BATCH = 8
SEQ = 4096
DEPTH = 2

BRANCH_WIDTH = 512
N_BRANCH = 4
GDN_HEADS = 4
GDN_DK = 128
GDN_DV = 128
GDN_CONV = 4
GDN_CHUNK = 64
GDN_QK = GDN_HEADS * GDN_DK
GDN_V = GDN_HEADS * GDN_DV
CONF_WIDTH = BRANCH_WIDTH
CONF_KERNEL = 31
GMLP_WIDTH = BRANCH_WIDTH
GMLP_GROUPS = 4
GMLP_CHUNK = 128
SC_WIDTH = BRANCH_WIDTH
SC_KERNEL = 3
MOE_GROUPS = 8
MOE_PER_GROUP = 8
N_EXPERTS = MOE_GROUPS * MOE_PER_GROUP
MOE_TOPK = 2
D_EXPERT = 512
MOE_BLOCK = 256
EPS = 1e-6

SPLIT_SIZES = (GDN_QK, GDN_QK, GDN_V, GDN_V, GDN_HEADS, GDN_HEADS,
               2 * CONF_WIDTH, 2 * GMLP_WIDTH, 3 * SC_WIDTH, N_BRANCH * D_MODEL)
SPLIT_POINTS = tuple(int(p) for p in np.cumsum(SPLIT_SIZES)[:-1])
D_IN = int(sum(SPLIT_SIZES))

kernel_name = 'hybrid_gated_parallel_mixers_hier_moe'


def rmsnorm(x, w):
    xf = x.astype(jnp.float32)
    y = xf * lax.rsqrt(jnp.mean(xf * xf, -1, keepdims=True) + EPS)
    return (y * w.astype(jnp.float32)).astype(x.dtype)


def layernorm(x, w, b):
    xf = x.astype(jnp.float32)
    xc = xf - jnp.mean(xf, -1, keepdims=True)
    y = xc * lax.rsqrt(jnp.mean(xc * xc, -1, keepdims=True) + EPS)
    return (y * w.astype(jnp.float32) + b.astype(jnp.float32)).astype(x.dtype)


def l2norm(t):
    return t * lax.rsqrt(jnp.sum(t * t, -1, keepdims=True) + EPS)


def causal_dwconv(x, w):
    k_w, ch = w.shape
    return lax.conv_general_dilated(
        x, w[:, None, :].astype(x.dtype), window_strides=(1,), padding=[(k_w - 1, 0)],
        dimension_numbers=('NWC', 'WIO', 'NWC'), feature_group_count=ch)


def gated_delta_rule(q, k, v, g, beta):
    bsz, nh, t_len, dk = q.shape
    dv = v.shape[-1]
    c = GDN_CHUNK
    n = t_len // c
    q = q.reshape(bsz, nh, n, c, dk)
    k = k.reshape(bsz, nh, n, c, dk)
    v = v.reshape(bsz, nh, n, c, dv)
    g = jnp.cumsum(g.reshape(bsz, nh, n, c), axis=-1)
    beta = beta.reshape(bsz, nh, n, c)
    tril = jnp.tril(jnp.ones((c, c), bool))
    strict = jnp.tril(jnp.ones((c, c), bool), -1)
    diff = g[..., :, None] - g[..., None, :]
    decay = jnp.where(tril, jnp.exp(jnp.where(tril, diff, 0.0)), 0.0)
    k_beta = k * beta[..., None]
    v_beta = v * beta[..., None]
    lower = jnp.einsum('bhnid,bhnjd->bhnij', k_beta, k) * decay
    m = jnp.where(strict, lower, 0.0) + jnp.eye(c, dtype=q.dtype)
    u = lax.linalg.triangular_solve(m, v_beta, left_side=True, lower=True, unit_diagonal=True)
    w = lax.linalg.triangular_solve(m, k_beta * jnp.exp(g)[..., None], left_side=True,
                                    lower=True, unit_diagonal=True)
    attn = jnp.einsum('bhnid,bhnjd->bhnij', q, k) * decay
    q_dec = q * jnp.exp(g)[..., None]
    g_last = g[..., -1]
    k_dec = k * jnp.exp(g_last[..., None] - g)[..., None]

    def step(s, xs):
        qd, kd, uc, wc, ac, gl = xs
        v_new = uc - jnp.einsum('bhcd,bhde->bhce', wc, s)
        o = jnp.einsum('bhcd,bhde->bhce', qd, s) + jnp.einsum('bhij,bhje->bhie', ac, v_new)
        s = s * jnp.exp(gl)[..., None, None] + jnp.einsum('bhcd,bhce->bhde', kd, v_new)
        return s, o

    xs = tuple(jnp.moveaxis(t, 2, 0) for t in (q_dec, k_dec, u, w, attn, g_last))
    s0 = jnp.zeros((bsz, nh, dk, dv), jnp.float32)
    _, o = lax.scan(step, s0, xs)
    return jnp.moveaxis(o, 0, 2).reshape(bsz, nh, t_len, dv)


def gdn_branch(q, k, v, z, beta_raw, a_raw, conv_w, a_log, dt_bias, norm_w):
    bsz, t_len, _ = q.shape
    dtype = q.dtype
    qkv = jax.nn.silu(causal_dwconv(jnp.concatenate([q, k, v], -1), conv_w))
    q, k, v = jnp.split(qkv, [GDN_QK, 2 * GDN_QK], -1)

    def heads(t, d):
        return t.reshape(bsz, t_len, GDN_HEADS, d).transpose(0, 2, 1, 3).astype(jnp.float32)

    q = l2norm(heads(q, GDN_DK)) * (GDN_DK ** -0.5)
    k = l2norm(heads(k, GDN_DK))
    v = heads(v, GDN_DV)
    beta = jax.nn.sigmoid(beta_raw.astype(jnp.float32)).transpose(0, 2, 1)
    g = -jnp.exp(a_log.astype(jnp.float32)) * jax.nn.softplus(
        a_raw.astype(jnp.float32) + dt_bias.astype(jnp.float32))
    g = g.transpose(0, 2, 1)
    o = gated_delta_rule(q, k, v, g, beta).transpose(0, 2, 1, 3)
    o = o * lax.rsqrt(jnp.mean(o * o, -1, keepdims=True) + EPS) * norm_w.astype(jnp.float32)
    o = o * jax.nn.silu(z.reshape(bsz, t_len, GDN_HEADS, GDN_DV).astype(jnp.float32))
    return o.reshape(bsz, t_len, GDN_V).astype(dtype)


def conformer_branch(h, conv_w, conv_b, ln_w, ln_b):
    a, gate = jnp.split(h, 2, -1)
    y = a * jax.nn.sigmoid(gate)
    y = causal_dwconv(y, conv_w) + conv_b.astype(h.dtype)
    return jax.nn.silu(layernorm(y, ln_w, ln_b))


def gmlp_branch(h, ln_w, ln_b, w_s, b_s):
    bsz, t_len, _ = h.shape
    u, v = jnp.split(jax.nn.gelu(h), 2, -1)
    v = layernorm(v, ln_w, ln_b)
    n = t_len // GMLP_CHUNK
    gc = GMLP_WIDTH // GMLP_GROUPS
    v = v.reshape(bsz, n, GMLP_CHUNK, GMLP_GROUPS, gc)
    mask = jnp.tril(jnp.ones((GMLP_CHUNK, GMLP_CHUNK), bool))
    w_c = jnp.where(mask[None], w_s, 0.0).astype(h.dtype)
    mixed = jnp.einsum('gij,bnjgc->bnigc', w_c, v) + b_s.T.astype(h.dtype)[None, None, :, :, None]
    return u * mixed.reshape(bsz, t_len, GMLP_WIDTH)


def shortconv_branch(h, conv_w):
    b_gate, c_gate, xin = jnp.split(h, 3, -1)
    return b_gate * causal_dwconv(c_gate * xin, conv_w)


def mixer_block(x, norm_w, w_in, gdn_conv_w, gdn_a_log, gdn_dt_bias, gdn_norm_w,
                conf_conv_w, conf_conv_b, conf_ln_w, conf_ln_b, gmlp_ln_w, gmlp_ln_b,
                gmlp_w_s, gmlp_b_s, sc_conv_w, w_branch, w_out):
    bsz, t_len, _ = x.shape
    xn = rmsnorm(x, norm_w)
    h = xn @ w_in
    q, k, v, z, beta_raw, a_raw, h_conf, h_gmlp, h_sc, h_gate = jnp.split(h, SPLIT_POINTS, -1)
    ys = (gdn_branch(q, k, v, z, beta_raw, a_raw, gdn_conv_w, gdn_a_log, gdn_dt_bias, gdn_norm_w),
          conformer_branch(h_conf, conf_conv_w, conf_conv_b, conf_ln_w, conf_ln_b),
          gmlp_branch(h_gmlp, gmlp_ln_w, gmlp_ln_b, gmlp_w_s, gmlp_b_s),
          shortconv_branch(h_sc, sc_conv_w))
    gates = jax.nn.sigmoid(h_gate.reshape(bsz, t_len, N_BRANCH, D_MODEL))
    merged = gates[..., 0, :] * (ys[0] @ w_branch[0])
    for i in range(1, N_BRANCH):
        merged = merged + gates[..., i, :] * (ys[i] @ w_branch[i])
    return merged @ w_out


def hier_moe(x, router_grp, router_exp, w_gate, w_up, w_down):
    bsz, t_len, d = x.shape
    xf = x.reshape(-1, d)
    n_tok = xf.shape[0]
    p_grp = jax.nn.softmax((xf @ router_grp).astype(jnp.float32), -1)
    grp = jnp.argmax(p_grp, -1).astype(jnp.int32)
    p_sel = jnp.take_along_axis(p_grp, grp[:, None], -1)
    logits = (xf @ router_exp).astype(jnp.float32).reshape(n_tok, MOE_GROUPS, MOE_PER_GROUP)
    logits_g = jnp.take_along_axis(logits, grp[:, None, None], 1)[:, 0]
    w_top, idx_top = lax.top_k(jax.nn.softmax(logits_g, -1), MOE_TOPK)
    w_top = w_top / jnp.sum(w_top, -1, keepdims=True) * p_sel
    expert = grp[:, None] * MOE_PER_GROUP + idx_top.astype(jnp.int32)
    n_assign = n_tok * MOE_TOPK
    e_flat = expert.reshape(-1)
    tok_flat = jnp.repeat(jnp.arange(n_tok, dtype=jnp.int32), MOE_TOPK)
    w_flat = w_top.reshape(-1)
    order = jnp.argsort(e_flat)
    e_sorted = e_flat[order]
    counts = jnp.zeros((N_EXPERTS,), jnp.int32).at[e_flat].add(1)
    padded = (counts + MOE_BLOCK - 1) // MOE_BLOCK * MOE_BLOCK
    pad_end = jnp.cumsum(padded)
    pad_start = pad_end - padded
    start = jnp.cumsum(counts) - counts
    dest = pad_start[e_sorted] + jnp.arange(n_assign, dtype=jnp.int32) - start[e_sorted]
    n_blocks = (n_assign + N_EXPERTS * (MOE_BLOCK - 1)) // MOE_BLOCK + 1
    n_rows = n_blocks * MOE_BLOCK
    buf_tok = jnp.full((n_rows,), n_tok, jnp.int32).at[dest].set(tok_flat[order])
    buf_w = jnp.zeros((n_rows,), jnp.float32).at[dest].set(w_flat[order])
    blk_start = jnp.arange(n_blocks, dtype=jnp.int32) * MOE_BLOCK
    blk_expert = jnp.minimum(jnp.searchsorted(pad_end, blk_start, side='right'),
                             N_EXPERTS - 1).astype(jnp.int32)
    x_pad = jnp.concatenate([xf, jnp.zeros((1, d), xf.dtype)], 0)

    def run_block(args):
        toks, e = args
        xb = x_pad[toks]
        hb = jax.nn.silu(xb @ w_gate[e]) * (xb @ w_up[e])
        return hb @ w_down[e]

    y = lax.map(run_block, (buf_tok.reshape(n_blocks, MOE_BLOCK), blk_expert))
    y = y.reshape(n_rows, d) * buf_w[:, None].astype(x.dtype)
    out = jnp.zeros((n_tok + 1, d), x.dtype).at[buf_tok].add(y)[:n_tok]
    return out.reshape(bsz, t_len, d)


def setup_inputs(seed: int = 0) -> dict:
    key = jax.random.key(seed)
    ks = jax.random.split(key, 26)
    f32 = jnp.float32
    L = DEPTH

    def nrm(k, shape, scale):
        return jax.random.normal(k, shape, f32) * scale

    def gain(k, shape):
        return 1.0 + 0.02 * jax.random.normal(k, shape, f32)

    dt = jnp.exp(jax.random.uniform(ks[4], (L, GDN_HEADS), f32, np.float32(np.log(1e-3)),
                                    np.float32(np.log(1e-1))))
    return {
        'x': nrm(ks[0], (BATCH, SEQ, D_MODEL), 1.0),
        'mix_norm_w': gain(ks[1], (L, D_MODEL)),
        'w_in': nrm(ks[2], (L, D_MODEL, D_IN), D_MODEL ** -0.5),
        'gdn_conv_w': nrm(ks[3], (L, GDN_CONV, 2 * GDN_QK + GDN_V), GDN_CONV ** -0.5),
        'gdn_A_log': jnp.log(jax.random.uniform(ks[5], (L, GDN_HEADS), f32, 1.0, 16.0)),
        'gdn_dt_bias': dt + jnp.log(-jnp.expm1(-dt)),
        'gdn_norm_w': gain(ks[6], (L, GDN_DV)),
        'conf_conv_w': nrm(ks[7], (L, CONF_KERNEL, CONF_WIDTH), CONF_KERNEL ** -0.5),
        'conf_conv_b': nrm(ks[8], (L, CONF_WIDTH), 0.02),
        'conf_ln_w': gain(ks[9], (L, CONF_WIDTH)),
        'conf_ln_b': nrm(ks[10], (L, CONF_WIDTH), 0.02),
        'gmlp_ln_w': gain(ks[11], (L, GMLP_WIDTH)),
        'gmlp_ln_b': nrm(ks[12], (L, GMLP_WIDTH), 0.02),
        'gmlp_w_s': nrm(ks[13], (L, GMLP_GROUPS, GMLP_CHUNK, GMLP_CHUNK), GMLP_CHUNK ** -0.5),
        'gmlp_b_s': gain(ks[14], (L, GMLP_GROUPS, GMLP_CHUNK)),
        'sc_conv_w': nrm(ks[15], (L, SC_KERNEL, SC_WIDTH), SC_KERNEL ** -0.5),
        'w_branch': nrm(ks[16], (L, N_BRANCH, BRANCH_WIDTH, D_MODEL), BRANCH_WIDTH ** -0.5),
        'w_out': nrm(ks[17], (L, D_MODEL, D_MODEL), D_MODEL ** -0.5),
        'moe_norm_w': gain(ks[18], (L, D_MODEL)),
        'router_grp': nrm(ks[19], (L, D_MODEL, MOE_GROUPS), D_MODEL ** -0.5),
        'router_exp': nrm(ks[20], (L, D_MODEL, N_EXPERTS), D_MODEL ** -0.5),
        'w_gate': nrm(ks[21], (L, N_EXPERTS, D_MODEL, D_EXPERT), D_MODEL ** -0.5),
        'w_up': nrm(ks[22], (L, N_EXPERTS, D_MODEL, D_EXPERT), D_MODEL ** -0.5),
        'w_down': nrm(ks[23], (L, N_EXPERTS, D_EXPERT, D_MODEL), D_EXPERT ** -0.5),
        'final_norm_w': gain(ks[24], (D_MODEL,)),
    }


def reference(x, mix_norm_w, w_in, gdn_conv_w, gdn_A_log, gdn_dt_bias, gdn_norm_w,
              conf_conv_w, conf_conv_b, conf_ln_w, conf_ln_b, gmlp_ln_w, gmlp_ln_b,
              gmlp_w_s, gmlp_b_s, sc_conv_w, w_branch, w_out, moe_norm_w, router_grp,
              router_exp, w_gate, w_up, w_down, final_norm_w):
    for l in range(DEPTH):
        x = x + mixer_block(x, mix_norm_w[l], w_in[l], gdn_conv_w[l], gdn_A_log[l],
                            gdn_dt_bias[l], gdn_norm_w[l], conf_conv_w[l], conf_conv_b[l],
                            conf_ln_w[l], conf_ln_b[l], gmlp_ln_w[l], gmlp_ln_b[l],
                            gmlp_w_s[l], gmlp_b_s[l], sc_conv_w[l], w_branch[l], w_out[l])
        x = x + hier_moe(rmsnorm(x, moe_norm_w[l]), router_grp[l], router_exp[l],
                         w_gate[l], w_up[l], w_down[l])
    return rmsnorm(x, final_norm_w)
```

```python
import functools

import jax
import jax.numpy as jnp
from jax import lax
from jax.experimental import pallas as pl
from jax.experimental.pallas import tpu as pltpu

F32 = jnp.float32
BF16 = jnp.bfloat16
I32 = jnp.int32

D_MODEL = 1024
BRANCH_WIDTH = 512
N_BRANCH = 4
GDN_HEADS = 4
GDN_DK = 128
GDN_DV = 128
GDN_CONV = 4
GDN_CHUNK = 64
GDN_QK = GDN_HEADS * GDN_DK
GDN_V = GDN_HEADS * GDN_DV
CONF_KERNEL = 31
GMLP_GROUPS = 4
GMLP_CHUNK = 128
SC_KERNEL = 3
MOE_GROUPS = 8
MOE_PER_GROUP = 8
N_EXPERTS = MOE_GROUPS * MOE_PER_GROUP
D_EXPERT = 512
MOE_BLOCK = 256
EPS = 1e-6

LANES = 128
SUBLANES = 8
VMEM_LIMIT = 56 * 1024 * 1024

SEQ_TILE = 512
ROW_TILE = 512
CONF_HALO = 32
SMALL_HALO = SUBLANES
CONV_ROWS = 64

INFO_E0, INFO_E1, INFO_W0, INFO_W1, INFO_R0, INFO_R1 = range(6)
GRP_LANE0 = N_EXPERTS


def _params(sem):
    return pltpu.CompilerParams(dimension_semantics=sem, vmem_limit_bytes=VMEM_LIMIT)


def _full(shape):
    nd = len(shape)
    return pl.BlockSpec(shape, lambda *_: (0,) * nd)


def _bdot(a, b):
    return jnp.dot(a.astype(BF16), b.astype(BF16), preferred_element_type=F32)


def _sigmoid(x):
    return 1.0 / (1.0 + jnp.exp(-x))


def _silu(x):
    return x * _sigmoid(x)


def _rms_body(x_ref, w_ref, o_ref):
    x = x_ref[...]
    ms = jnp.mean(x * x, -1, keepdims=True)
    o_ref[...] = (x * lax.rsqrt(ms + EPS) * w_ref[...]).astype(o_ref.dtype)


def _rmsnorm(x2d, w, out_dtype):
    n, d = x2d.shape
    tm = min(ROW_TILE, n)
    return pl.pallas_call(
        _rms_body,
        grid=(n // tm,),
        in_specs=[pl.BlockSpec((tm, d), lambda i: (i, 0)), _full((1, d))],
        out_specs=pl.BlockSpec((tm, d), lambda i: (i, 0)),
        out_shape=jax.ShapeDtypeStruct((n, d), out_dtype),
        compiler_params=_params(("parallel",)),
        name="rmsnorm",
    )(x2d, w.reshape(1, d))


def _carry_halo(buf, halo, tile):
    t = pl.program_id(1)

    @pl.when(t == 0)
    def _():
        buf[0:halo, :] = jnp.zeros((halo, buf.shape[1]), buf.dtype)

    @pl.when(t != 0)
    def _():
        buf[0:halo, :] = buf[tile:tile + halo, :]


def _causal_taps(buf, cw_ref, halo, row0, rows, init):
    k_w = cw_ref.shape[0]
    acc = init
    for k in range(k_w):
        off = halo - (k_w - 1) + k + row0
        acc = acc + cw_ref[k:k + 1, :] * buf[off:off + rows, :]
    return acc


def _layernorm(x, w, b):
    mu = jnp.mean(x, -1, keepdims=True)
    xc = x - mu
    var = jnp.mean(xc * xc, -1, keepdims=True)
    return xc * lax.rsqrt(var + EPS) * w + b


def _seq_call(body, xn, weights, out_width, scratch, name):
    bsz, t_len, d = xn.shape
    tile = min(SEQ_TILE, t_len)
    in_specs = [pl.BlockSpec((None, tile, d), lambda b, t: (b, t, 0))]
    in_specs += [_full(w.shape) for w in weights]
    return pl.pallas_call(
        body,
        grid=(bsz, t_len // tile),
        in_specs=in_specs,
        out_specs=pl.BlockSpec((None, tile, out_width), lambda b, t: (b, t, 0)),
        out_shape=jax.ShapeDtypeStruct((bsz, t_len, out_width), BF16),
        scratch_shapes=scratch(tile),
        compiler_params=_params(("parallel", "arbitrary")),
        name=name,
    )(xn, *weights)


def _conf_body(xn_ref, w_ref, cw_ref, cb_ref, lw_ref, lb_ref, o_ref, ybuf):
    tile, width = o_ref.shape
    _carry_halo(ybuf, CONF_HALO, tile)
    h = jnp.dot(xn_ref[...], w_ref[...], preferred_element_type=F32)
    ybuf[CONF_HALO:CONF_HALO + tile, :] = h[:, :width] * _sigmoid(h[:, width:])
    for r in range(tile // CONV_ROWS):
        init = jnp.broadcast_to(cb_ref[...], (CONV_ROWS, width))
        y = _causal_taps(ybuf, cw_ref, CONF_HALO, r * CONV_ROWS, CONV_ROWS, init)
        y = _layernorm(y, lw_ref[...], lb_ref[...])
        o_ref[r * CONV_ROWS:(r + 1) * CONV_ROWS, :] = _silu(y).astype(o_ref.dtype)


def _conformer(xn, w, conv_w, conv_b, ln_w, ln_b):
    width = BRANCH_WIDTH
    weights = [w, conv_w, conv_b.reshape(1, width), ln_w.reshape(1, width), ln_b.reshape(1, width)]
    scratch = lambda tile: [pltpu.VMEM((CONF_HALO + tile, width), F32)]
    return _seq_call(_conf_body, xn, weights, width, scratch, "conformer")


def _sc_body(xn_ref, w_ref, cw_ref, o_ref, ubuf, bbuf):
    tile, width = o_ref.shape
    _carry_halo(ubuf, SMALL_HALO, tile)
    h = jnp.dot(xn_ref[...], w_ref[...], preferred_element_type=F32)
    bbuf[...] = h[:, :width]
    ubuf[SMALL_HALO:SMALL_HALO + tile, :] = h[:, width:2 * width] * h[:, 2 * width:]
    for r in range(tile // CONV_ROWS):
        rows = slice(r * CONV_ROWS, (r + 1) * CONV_ROWS)
        init = jnp.zeros((CONV_ROWS, width), F32)
        y = _causal_taps(ubuf, cw_ref, SMALL_HALO, r * CONV_ROWS, CONV_ROWS, init)
        o_ref[rows, :] = (bbuf[rows, :] * y).astype(o_ref.dtype)


def _shortconv(xn, w, conv_w):
    width = BRANCH_WIDTH
    scratch = lambda tile: [pltpu.VMEM((SMALL_HALO + tile, width), F32), pltpu.VMEM((tile, width), F32)]
    return _seq_call(_sc_body, xn, [w, conv_w], width, scratch, "shortconv")


def _gmlp_body(xn_ref, w_ref, lw_ref, lb_ref, ws_ref, bs_ref, o_ref, ubuf, vbuf):
    tile, width = o_ref.shape
    gc = width // GMLP_GROUPS
    h = jnp.dot(xn_ref[...], w_ref[...], preferred_element_type=F32)
    g = jax.nn.gelu(h)
    ubuf[...] = g[:, :width]
    vbuf[...] = _layernorm(g[:, width:], lw_ref[...], lb_ref[...]).astype(BF16)
    row = lax.broadcasted_iota(I32, (GMLP_CHUNK, GMLP_CHUNK), 0)
    col = lax.broadcasted_iota(I32, (GMLP_CHUNK, GMLP_CHUNK), 1)
    for gi in range(GMLP_GROUPS):
        w_c = jnp.where(row >= col, ws_ref[gi], 0.0).astype(BF16)
        cols = slice(gi * gc, (gi + 1) * gc)
        for c in range(tile // GMLP_CHUNK):
            rows = slice(c * GMLP_CHUNK, (c + 1) * GMLP_CHUNK)
            mixed = jnp.dot(w_c, vbuf[rows, cols], preferred_element_type=F32) + bs_ref[gi]
            o_ref[rows, cols] = (ubuf[rows, cols] * mixed).astype(o_ref.dtype)


def _gmlp(xn, w, ln_w, ln_b, w_s, b_s):
    width = BRANCH_WIDTH
    gc = width // GMLP_GROUPS
    b_full = jnp.broadcast_to(b_s[:, :, None], (GMLP_GROUPS, GMLP_CHUNK, gc))
    weights = [w, ln_w.reshape(1, width), ln_b.reshape(1, width), w_s, b_full]
    scratch = lambda tile: [pltpu.VMEM((tile, width), F32), pltpu.VMEM((tile, width), BF16)]
    return _seq_call(_gmlp_body, xn, weights, width, scratch, "gmlp")


def _softplus(x):
    return jnp.maximum(x, 0.0) + jnp.log(1.0 + jnp.exp(-jnp.abs(x)))


def _unit_lower_inverse(strict_lower, eye):
    size = strict_lower.shape[0]
    inv = eye - strict_lower
    power = strict_lower
    span = 2
    while span < size:
        power = _bdot(power, power)
        inv = inv + _bdot(inv, power)
        span *= 2
    return inv


def _gdn_body(xn_ref, w_ref, wb_ref, wa_ref, cw_ref, alog_ref, dtb_ref, nw_ref, o_ref,
              hbuf, qbuf, kbuf, vbuf, zbuf, gbuf, bbuf, s_ref):
    tile = o_ref.shape[0]
    chunk = GDN_CHUNK
    t = pl.program_id(1)
    _carry_halo(hbuf, SMALL_HALO, tile)

    @pl.when(t == 0)
    def _():
        s_ref[...] = jnp.zeros(s_ref.shape, F32)

    xn = xn_ref[...]
    h = jnp.dot(xn, w_ref[...], preferred_element_type=F32)
    n_conv = 2 * GDN_QK + GDN_V
    hbuf[SMALL_HALO:SMALL_HALO + tile, :] = h[:, :n_conv]
    zbuf[...] = h[:, n_conv:]
    bbuf[...] = _sigmoid(jnp.dot(xn, wb_ref[...], preferred_element_type=F32))
    a_raw = jnp.dot(xn, wa_ref[...], preferred_element_type=F32)
    gbuf[...] = -jnp.exp(alog_ref[...]) * _softplus(a_raw + dtb_ref[...])

    for r in range(tile // CONV_ROWS):
        rows = slice(r * CONV_ROWS, (r + 1) * CONV_ROWS)
        init = jnp.zeros((CONV_ROWS, n_conv), F32)
        qkv = _silu(_causal_taps(hbuf, cw_ref, SMALL_HALO, r * CONV_ROWS, CONV_ROWS, init))
        for hd in range(GDN_HEADS):
            q = qkv[:, hd * GDN_DK:(hd + 1) * GDN_DK]
            k = qkv[:, GDN_QK + hd * GDN_DK:GDN_QK + (hd + 1) * GDN_DK]
            q = q * lax.rsqrt(jnp.sum(q * q, -1, keepdims=True) + EPS) * (GDN_DK ** -0.5)
            k = k * lax.rsqrt(jnp.sum(k * k, -1, keepdims=True) + EPS)
            qbuf[rows, hd * GDN_DK:(hd + 1) * GDN_DK] = q
            kbuf[rows, hd * GDN_DK:(hd + 1) * GDN_DK] = k
        vbuf[rows, :] = qkv[:, 2 * GDN_QK:]

    row = lax.broadcasted_iota(I32, (chunk, chunk), 0)
    col = lax.broadcasted_iota(I32, (chunk, chunk), 1)
    tril = row >= col
    strict = row > col
    tril_f = tril.astype(F32)
    eye = (row == col).astype(F32)
    nt_dims = (((1,), (1,)), ((), ()))
    tn_dims = (((0,), (0,)), ((), ()))

    def chunk_step(c, carry):
        r0 = pl.multiple_of(c * chunk, chunk)
        rows = pl.ds(r0, chunk)
        g_cum = jnp.dot(tril_f, gbuf[rows, :], preferred_element_type=F32,
                        precision=lax.Precision.HIGHEST)
        g_cum_t = g_cum.T
        beta = bbuf[rows, :]
        for hd in range(GDN_HEADS):
            cols = slice(hd * GDN_DK, (hd + 1) * GDN_DK)
            q = qbuf[rows, cols]
            k = kbuf[rows, cols]
            v = vbuf[rows, cols]
            g_col = g_cum[:, hd:hd + 1]
            g_row = g_cum_t[hd:hd + 1, :]
            g_last = g_cum[chunk - 1:chunk, hd:hd + 1]
            b_col = beta[:, hd:hd + 1]
            decay = jnp.where(tril, jnp.exp(jnp.where(tril, g_col - g_row, 0.0)), 0.0)
            e_g = jnp.exp(g_col)
            k_beta = k * b_col
            v_beta = v * b_col
            k16 = k.astype(BF16)
            kk = lax.dot_general(k_beta.astype(BF16), k16, nt_dims, preferred_element_type=F32)
            inv = _unit_lower_inverse(jnp.where(strict, kk * decay, 0.0), eye)
            uw = _bdot(inv, jnp.concatenate([v_beta, k_beta * e_g], axis=1))
            u = uw[:, :GDN_DV]
            w = uw[:, GDN_DV:]
            attn = lax.dot_general(q.astype(BF16), k16, nt_dims, preferred_element_type=F32) * decay
            state = s_ref[hd]
            s16 = state.astype(BF16)
            v_new = u - _bdot(w, s16)
            v16 = v_new.astype(BF16)
            o = _bdot(q * e_g, s16) + _bdot(attn, v16)
            k_dec = k * jnp.exp(g_last - g_col)
            s_ref[hd] = state * jnp.exp(g_last) + lax.dot_general(
                k_dec.astype(BF16), v16, tn_dims, preferred_element_type=F32)
            o = o * lax.rsqrt(jnp.mean(o * o, -1, keepdims=True) + EPS) * nw_ref[...]
            o_ref[rows, cols] = (o * _silu(zbuf[rows, cols])).astype(o_ref.dtype)
        return carry

    lax.fori_loop(0, tile // chunk, chunk_step, 0)


def _lane_pad(vec):
    return jnp.zeros((1, LANES), F32).at[0, :vec.shape[0]].set(vec)


def _gdn(xn, w, w_beta, w_a, conv_w, a_log, dt_bias, norm_w):
    pad = ((0, 0), (0, LANES - GDN_HEADS))
    weights = [w, jnp.pad(w_beta, pad), jnp.pad(w_a, pad), conv_w,
               _lane_pad(a_log), _lane_pad(dt_bias), norm_w.reshape(1, GDN_DV)]
    n_conv = 2 * GDN_QK + GDN_V

    def scratch(tile):
        return [pltpu.VMEM((SMALL_HALO + tile, n_conv), F32),
                pltpu.VMEM((tile, GDN_QK), F32), pltpu.VMEM((tile, GDN_QK), F32),
                pltpu.VMEM((tile, GDN_V), F32), pltpu.VMEM((tile, GDN_V), F32),
                pltpu.VMEM((tile, LANES), F32), pltpu.VMEM((tile, LANES), F32),
                pltpu.VMEM((GDN_HEADS, GDN_DK, GDN_DV), F32)]

    return _seq_call(_gdn_body, xn, weights, GDN_V, scratch, "gdn")


def _merge_body(x_ref, xn_ref, y0_ref, y1_ref, y2_ref, y3_ref, wg_ref, wb_ref, wo_ref, o_ref):
    xn = xn_ref[...]
    merged = None
    for i, y_ref in enumerate((y0_ref, y1_ref, y2_ref, y3_ref)):
        gate = _sigmoid(jnp.dot(xn, wg_ref[i], preferred_element_type=F32))
        term = gate * jnp.dot(y_ref[...], wb_ref[i], preferred_element_type=F32)
        merged = term if merged is None else merged + term
    o_ref[...] = x_ref[...] + jnp.dot(merged.astype(BF16), wo_ref[...], preferred_element_type=F32)


def _merge(x2d, xn2d, ys, w_gate, w_branch, w_out):
    n, d = x2d.shape
    tm = min(ROW_TILE, n)
    row = lambda width: pl.BlockSpec((tm, width), lambda i: (i, 0))
    return pl.pallas_call(
        _merge_body,
        grid=(n // tm,),
        in_specs=[row(d), row(d)] + [row(BRANCH_WIDTH)] * N_BRANCH
        + [_full(w_gate.shape), _full(w_branch.shape), _full(w_out.shape)],
        out_specs=row(d),
        out_shape=jax.ShapeDtypeStruct((n, d), F32),
        compiler_params=_params(("parallel",)),
        name="merge",
    )(x2d, xn2d, *ys, w_gate, w_branch, w_out)


def _split2(x):
    hi = x.astype(BF16)
    lo = (x - hi.astype(F32)).astype(BF16)
    return hi, lo


def _router_body(x_ref, nw_ref, wr_ref, xn_ref, info_ref, cnt_ref):
    tm = x_ref.shape[0]

    @pl.when(pl.program_id(0) == 0)
    def _():
        cnt_ref[...] = jnp.zeros(cnt_ref.shape, F32)

    x = x_ref[...]
    xn = x * lax.rsqrt(jnp.mean(x * x, -1, keepdims=True) + EPS) * nw_ref[...]
    xn_ref[...] = xn
    x_hi, x_lo = _split2(xn)
    logits = (jnp.dot(x_lo, wr_ref[0], preferred_element_type=F32)
              + jnp.dot(x_hi, wr_ref[1], preferred_element_type=F32)
              + jnp.dot(x_hi, wr_ref[0], preferred_element_type=F32))

    lane = lax.broadcasted_iota(I32, (tm, LANES), 1)
    neg = jnp.float32(-jnp.inf)
    big = jnp.int32(2 * LANES)
    is_grp = (lane >= GRP_LANE0) & (lane < GRP_LANE0 + MOE_GROUPS)
    lg = jnp.where(is_grp, logits, neg)
    mg = jnp.max(lg, -1, keepdims=True)
    grp = jnp.min(jnp.where(lg == mg, lane, big), -1, keepdims=True) - GRP_LANE0
    p_sel = 1.0 / jnp.sum(jnp.where(is_grp, jnp.exp(lg - mg), 0.0), -1, keepdims=True)

    in_grp = (lane < N_EXPERTS) & ((lane // MOE_PER_GROUP) == grp)
    le = jnp.where(in_grp, logits, neg)
    m0 = jnp.max(le, -1, keepdims=True)
    e0 = jnp.min(jnp.where(le == m0, lane, big), -1, keepdims=True)
    le1 = jnp.where(lane == e0, neg, le)
    m1 = jnp.max(le1, -1, keepdims=True)
    e1 = jnp.min(jnp.where(le1 == m1, lane, big), -1, keepdims=True)
    r = jnp.exp(m1 - m0)
    w0 = p_sel / (1.0 + r)
    w1 = p_sel * r / (1.0 + r)

    hot0 = lane == e0
    hot1 = lane == e1
    onehot = jnp.where(hot0 | hot1, 1.0, 0.0)
    ti = lax.broadcasted_iota(I32, (tm, tm), 0)
    tj = lax.broadcasted_iota(I32, (tm, tm), 1)
    before = jnp.where(ti > tj, 1.0, 0.0).astype(BF16)
    base = jnp.dot(before, onehot.astype(BF16), preferred_element_type=F32) + cnt_ref[...]
    r0 = jnp.sum(jnp.where(hot0, base, 0.0), -1, keepdims=True)
    r1 = jnp.sum(jnp.where(hot1, base, 0.0), -1, keepdims=True)
    cnt_ref[...] = cnt_ref[...] + jnp.sum(onehot, 0, keepdims=True)

    info = jnp.zeros((tm, LANES), F32)
    for idx, val in ((INFO_E0, e0.astype(F32)), (INFO_E1, e1.astype(F32)), (INFO_W0, w0),
                     (INFO_W1, w1), (INFO_R0, r0), (INFO_R1, r1)):
        info = jnp.where(lane == idx, val, info)
    info_ref[...] = info


def _router(x2d, norm_w, router_grp, router_exp):
    n, d = x2d.shape
    tm = min(ROW_TILE, n)
    w_r = jnp.zeros((d, LANES), F32).at[:, :N_EXPERTS].set(router_exp)
    w_r = w_r.at[:, GRP_LANE0:GRP_LANE0 + MOE_GROUPS].set(router_grp)
    w_parts = jnp.stack(_split2(w_r))
    return pl.pallas_call(
        _router_body,
        grid=(n // tm,),
        in_specs=[pl.BlockSpec((tm, d), lambda i: (i, 0)), _full((1, d)), _full(w_parts.shape)],
        out_specs=[pl.BlockSpec((tm, d), lambda i: (i, 0)),
                   pl.BlockSpec((tm, LANES), lambda i: (i, 0)),
                   _full((1, LANES))],
        out_shape=[jax.ShapeDtypeStruct((n, d), F32),
                   jax.ShapeDtypeStruct((n, LANES), F32),
                   jax.ShapeDtypeStruct((1, LANES), F32)],
        compiler_params=_params(("arbitrary",)),
        name="router",
    )(x2d, norm_w.reshape(1, d), w_parts)


def _row_copy(src, src_row, dst, dst_row, sem):
    return pltpu.make_async_copy(src.at[pl.ds(src_row, 1)], dst.at[pl.ds(dst_row, 1)], sem)


def _dispatch_body(dest_ref, x_ref, xs_in_ref, xs_ref, sem):
    del xs_in_ref
    tm = x_ref.shape[0]

    def issue(r, carry):
        _row_copy(x_ref, r, xs_ref, dest_ref[2 * r], sem).start()
        _row_copy(x_ref, r, xs_ref, dest_ref[2 * r + 1], sem).start()
        return carry

    def drain(r, carry):
        _row_copy(x_ref, r, xs_ref, dest_ref[2 * r], sem).wait()
        _row_copy(x_ref, r, xs_ref, dest_ref[2 * r + 1], sem).wait()
        return carry

    lax.fori_loop(0, tm, issue, 0)
    lax.fori_loop(0, tm, drain, 0)


def _dispatch(xn2d, dest, n_rows):
    n, d = xn2d.shape
    tm = min(ROW_TILE, n)
    zeros = jnp.zeros((n_rows, d), xn2d.dtype)
    return pl.pallas_call(
        _dispatch_body,
        grid=(n // tm,),
        in_specs=[pl.BlockSpec((2 * tm,), lambda i: (i,), memory_space=pltpu.SMEM),
                  pl.BlockSpec((tm, d), lambda i: (i, 0)),
                  pl.BlockSpec(memory_space=pl.ANY)],
        out_specs=pl.BlockSpec(memory_space=pl.ANY),
        out_shape=jax.ShapeDtypeStruct((n_rows, d), xn2d.dtype),
        scratch_shapes=[pltpu.SemaphoreType.DMA(())],
        input_output_aliases={2: 0},
        compiler_params=_params(("arbitrary",)),
        name="dispatch",
    )(dest, xn2d, zeros)


def _ffn_body(blk_ref, xs_ref, wg_ref, wu_ref, wd_ref, y_ref):
    del blk_ref
    xb = xs_ref[...].astype(BF16)
    g = jnp.dot(xb, wg_ref[...], preferred_element_type=F32)
    u = jnp.dot(xb, wu_ref[...], preferred_element_type=F32)
    hb = (_silu(g) * u).astype(BF16)
    y_ref[...] = jnp.dot(hb, wd_ref[...], preferred_element_type=F32)


def _ffn(xs, blk_expert, w_gate, w_up, w_down):
    n_rows, d = xs.shape
    n_blocks = n_rows // MOE_BLOCK
    de = w_gate.shape[-1]
    grid_spec = pltpu.PrefetchScalarGridSpec(
        num_scalar_prefetch=1,
        grid=(n_blocks,),
        in_specs=[pl.BlockSpec((MOE_BLOCK, d), lambda b, e: (b, 0)),
                  pl.BlockSpec((None, d, de), lambda b, e: (e[b], 0, 0)),
                  pl.BlockSpec((None, d, de), lambda b, e: (e[b], 0, 0)),
                  pl.BlockSpec((None, de, d), lambda b, e: (e[b], 0, 0))],
        out_specs=pl.BlockSpec((MOE_BLOCK, d), lambda b, e: (b, 0)),
    )
    return pl.pallas_call(
        _ffn_body,
        grid_spec=grid_spec,
        out_shape=jax.ShapeDtypeStruct((n_rows, d), F32),
        compiler_params=_params(("arbitrary",)),
        name="expert_ffn",
    )(blk_expert, xs, w_gate, w_up, w_down)


def _combine_body(dest_ref, x_ref, info_ref, y_ref, o_ref, gbuf, sem):
    tm = x_ref.shape[0]

    def issue(r, carry):
        _row_copy(y_ref, dest_ref[2 * r], gbuf.at[0], r, sem).start()
        _row_copy(y_ref, dest_ref[2 * r + 1], gbuf.at[1], r, sem).start()
        return carry

    def drain(r, carry):
        _row_copy(y_ref, dest_ref[2 * r], gbuf.at[0], r, sem).wait()
        _row_copy(y_ref, dest_ref[2 * r + 1], gbuf.at[1], r, sem).wait()
        return carry

    lax.fori_loop(0, tm, issue, 0)
    lax.fori_loop(0, tm, drain, 0)
    info = info_ref[...]
    w0 = info[:, INFO_W0:INFO_W0 + 1]
    w1 = info[:, INFO_W1:INFO_W1 + 1]
    o_ref[...] = x_ref[...] + w0 * gbuf[0] + w1 * gbuf[1]


def _combine(x2d, info, dest, y):
    n, d = x2d.shape
    tm = min(ROW_TILE, n)
    return pl.pallas_call(
        _combine_body,
        grid=(n // tm,),
        in_specs=[pl.BlockSpec((2 * tm,), lambda i: (i,), memory_space=pltpu.SMEM),
                  pl.BlockSpec((tm, d), lambda i: (i, 0)),
                  pl.BlockSpec((tm, LANES), lambda i: (i, 0)),
                  pl.BlockSpec(memory_space=pl.ANY)],
        out_specs=pl.BlockSpec((tm, d), lambda i: (i, 0)),
        out_shape=jax.ShapeDtypeStruct((n, d), F32),
        scratch_shapes=[pltpu.VMEM((2, tm, d), F32), pltpu.SemaphoreType.DMA(())],
        compiler_params=_params(("arbitrary",)),
        name="combine",
    )(dest, x2d, info, y)


def _hier_moe(x2d, norm_w, router_grp, router_exp, w_gate, w_up, w_down):
    n, _ = x2d.shape
    xn, info, cnt = _router(x2d, norm_w, router_grp, router_exp)
    counts = cnt[0, :N_EXPERTS].astype(I32)
    padded = (counts + MOE_BLOCK - 1) // MOE_BLOCK * MOE_BLOCK
    pad_end = jnp.cumsum(padded)
    pad_start = pad_end - padded
    expert = info[:, INFO_E0:INFO_E1 + 1].astype(I32)
    rank = info[:, INFO_R0:INFO_R1 + 1].astype(I32)
    dest = (pad_start[expert] + rank).reshape(-1)
    n_blocks = (2 * n + N_EXPERTS * (MOE_BLOCK - 1)) // MOE_BLOCK + 1
    blk_start = jnp.arange(n_blocks, dtype=I32) * MOE_BLOCK
    blk_expert = jnp.minimum(jnp.searchsorted(pad_end, blk_start, side="right"),
                             N_EXPERTS - 1).astype(I32)
    xs = _dispatch(xn, dest, n_blocks * MOE_BLOCK)
    y = _ffn(xs, blk_expert, w_gate.astype(BF16), w_up.astype(BF16), w_down.astype(BF16))
    return _combine(x2d, info, dest, y)


def _mixer_block(x, norm_w, w_in, gdn_conv_w, gdn_a_log, gdn_dt_bias, gdn_norm_w,
                 conf_conv_w, conf_conv_b, conf_ln_w, conf_ln_b, gmlp_ln_w, gmlp_ln_b,
                 gmlp_w_s, gmlp_b_s, sc_conv_w, w_branch, w_out):
    bsz, t_len, d = x.shape
    x2d = x.reshape(bsz * t_len, d)
    xn2d = _rmsnorm(x2d, norm_w, BF16)
    xn = xn2d.reshape(bsz, t_len, d)
    c0 = 2 * GDN_QK + 2 * GDN_V
    c1 = c0 + GDN_HEADS
    c2 = c1 + GDN_HEADS
    c3 = c2 + 2 * BRANCH_WIDTH
    c4 = c3 + 2 * BRANCH_WIDTH
    c5 = c4 + 3 * BRANCH_WIDTH
    w16 = w_in.astype(BF16)
    ys = (
        _gdn(xn, w16[:, :c0], w16[:, c0:c1], w16[:, c1:c2], gdn_conv_w, gdn_a_log, gdn_dt_bias,
             gdn_norm_w),
        _conformer(xn, w16[:, c2:c3], conf_conv_w, conf_conv_b, conf_ln_w, conf_ln_b),
        _gmlp(xn, w16[:, c3:c4], gmlp_ln_w, gmlp_ln_b, gmlp_w_s, gmlp_b_s),
        _shortconv(xn, w16[:, c4:c5], sc_conv_w),
    )
    ys = [y.reshape(bsz * t_len, BRANCH_WIDTH) for y in ys]
    w_gate = w16[:, c5:].reshape(d, N_BRANCH, d).transpose(1, 0, 2)
    return _merge(x2d, xn2d, ys, w_gate, w_branch.astype(BF16), w_out.astype(BF16))


def kernel(x, mix_norm_w, w_in, gdn_conv_w, gdn_A_log, gdn_dt_bias, gdn_norm_w, conf_conv_w,
           conf_conv_b, conf_ln_w, conf_ln_b, gmlp_ln_w, gmlp_ln_b, gmlp_w_s, gmlp_b_s, sc_conv_w,
           w_branch, w_out, moe_norm_w, router_grp, router_exp, w_gate, w_up, w_down, final_norm_w):
    bsz, t_len, d = x.shape
    for l in range(mix_norm_w.shape[0]):
        x2d = _mixer_block(x, mix_norm_w[l], w_in[l], gdn_conv_w[l], gdn_A_log[l], gdn_dt_bias[l],
                           gdn_norm_w[l], conf_conv_w[l], conf_conv_b[l], conf_ln_w[l],
                           conf_ln_b[l], gmlp_ln_w[l], gmlp_ln_b[l], gmlp_w_s[l], gmlp_b_s[l],
                           sc_conv_w[l], w_branch[l], w_out[l])
        x2d = _hier_moe(x2d, moe_norm_w[l], router_grp[l], router_exp[l], w_gate[l], w_up[l],
                        w_down[l])
        x = x2d.reshape(bsz, t_len, d)
    return _rmsnorm(x.reshape(bsz * t_len, d), final_norm_w, F32).reshape(bsz, t_len, d)
```

```python
import functools

import jax
import jax.numpy as jnp
from jax import lax
from jax.experimental import pallas as pl
from jax.experimental.pallas import tpu as pltpu

F32 = jnp.float32
BF16 = jnp.bfloat16
I32 = jnp.int32

D_MODEL = 1024
BRANCH_WIDTH = 512
N_BRANCH = 4
GDN_HEADS = 4
GDN_DK = 128
GDN_DV = 128
GDN_CONV = 4
GDN_CHUNK = 64
GDN_QK = GDN_HEADS * GDN_DK
GDN_V = GDN_HEADS * GDN_DV
CONF_KERNEL = 31
GMLP_GROUPS = 4
GMLP_CHUNK = 128
SC_KERNEL = 3
MOE_GROUPS = 8
MOE_PER_GROUP = 8
N_EXPERTS = MOE_GROUPS * MOE_PER_GROUP
D_EXPERT = 512
MOE_BLOCK = 256
EPS = 1e-6

LANES = 128
SUBLANES = 8
VMEM_LIMIT = 56 * 1024 * 1024

SEQ_TILE = 512
ROW_TILE = 512
CONF_HALO = 32
SMALL_HALO = SUBLANES
CONV_ROWS = 64

INFO_E0, INFO_E1, INFO_W0, INFO_W1, INFO_R0, INFO_R1 = range(6)
GRP_LANE0 = N_EXPERTS


def _params(sem):
    return pltpu.CompilerParams(dimension_semantics=sem, vmem_limit_bytes=VMEM_LIMIT)


def _full(shape):
    nd = len(shape)
    return pl.BlockSpec(shape, lambda *_: (0,) * nd)


def _bdot(a, b):
    return jnp.dot(a.astype(BF16), b.astype(BF16), preferred_element_type=F32)


def _sigmoid(x):
    return 1.0 / (1.0 + jnp.exp(-x))


def _silu(x):
    return x * _sigmoid(x)


def _rms_body(x_ref, w_ref, o_ref):
    x = x_ref[...]
    ms = jnp.mean(x * x, -1, keepdims=True)
    o_ref[...] = (x * lax.rsqrt(ms + EPS) * w_ref[...]).astype(o_ref.dtype)


def _rmsnorm(x2d, w, out_dtype):
    n, d = x2d.shape
    tm = min(ROW_TILE, n)
    return pl.pallas_call(
        _rms_body,
        grid=(n // tm,),
        in_specs=[pl.BlockSpec((tm, d), lambda i: (i, 0)), _full((1, d))],
        out_specs=pl.BlockSpec((tm, d), lambda i: (i, 0)),
        out_shape=jax.ShapeDtypeStruct((n, d), out_dtype),
        compiler_params=_params(("parallel",)),
        name="rmsnorm",
    )(x2d, w.reshape(1, d))


def _carry_halo(buf, halo, tile):
    t = pl.program_id(1)

    @pl.when(t == 0)
    def _():
        buf[0:halo, :] = jnp.zeros((halo, buf.shape[1]), buf.dtype)

    @pl.when(t != 0)
    def _():
        buf[0:halo, :] = buf[tile:tile + halo, :]


def _causal_taps(buf, cw_ref, halo, row0, rows, init):
    k_w = cw_ref.shape[0]
    acc = init
    for k in range(k_w):
        off = halo - (k_w - 1) + k + row0
        acc = acc + cw_ref[k:k + 1, :] * buf[off:off + rows, :]
    return acc


def _layernorm(x, w, b):
    mu = jnp.mean(x, -1, keepdims=True)
    xc = x - mu
    var = jnp.mean(xc * xc, -1, keepdims=True)
    return xc * lax.rsqrt(var + EPS) * w + b


def _seq_call(body, xn, weights, out_width, scratch, name):
    bsz, t_len, d = xn.shape
    tile = min(SEQ_TILE, t_len)
    in_specs = [pl.BlockSpec((None, tile, d), lambda b, t: (b, t, 0))]
    in_specs += [_full(w.shape) for w in weights]
    return pl.pallas_call(
        body,
        grid=(bsz, t_len // tile),
        in_specs=in_specs,
        out_specs=pl.BlockSpec((None, tile, out_width), lambda b, t: (b, t, 0)),
        out_shape=jax.ShapeDtypeStruct((bsz, t_len, out_width), BF16),
        scratch_shapes=scratch(tile),
        compiler_params=_params(("parallel", "arbitrary")),
        name=name,
    )(xn, *weights)


def _conf_body(xn_ref, w_ref, cw_ref, cb_ref, lw_ref, lb_ref, o_ref, ybuf):
    tile, width = o_ref.shape
    _carry_halo(ybuf, CONF_HALO, tile)
    h = jnp.dot(xn_ref[...], w_ref[...], preferred_element_type=F32)
    ybuf[CONF_HALO:CONF_HALO + tile, :] = h[:, :width] * _sigmoid(h[:, width:])
    for r in range(tile // CONV_ROWS):
        init = jnp.broadcast_to(cb_ref[...], (CONV_ROWS, width))
        y = _causal_taps(ybuf, cw_ref, CONF_HALO, r * CONV_ROWS, CONV_ROWS, init)
        y = _layernorm(y, lw_ref[...], lb_ref[...])
        o_ref[r * CONV_ROWS:(r + 1) * CONV_ROWS, :] = _silu(y).astype(o_ref.dtype)


def _conformer(xn, w, conv_w, conv_b, ln_w, ln_b):
    width = BRANCH_WIDTH
    weights = [w, conv_w, conv_b.reshape(1, width), ln_w.reshape(1, width), ln_b.reshape(1, width)]
    scratch = lambda tile: [pltpu.VMEM((CONF_HALO + tile, width), F32)]
    return _seq_call(_conf_body, xn, weights, width, scratch, "conformer")


def _sc_body(xn_ref, w_ref, cw_ref, o_ref, ubuf, bbuf):
    tile, width = o_ref.shape
    _carry_halo(ubuf, SMALL_HALO, tile)
    h = jnp.dot(xn_ref[...], w_ref[...], preferred_element_type=F32)
    bbuf[...] = h[:, :width]
    ubuf[SMALL_HALO:SMALL_HALO + tile, :] = h[:, width:2 * width] * h[:, 2 * width:]
    for r in range(tile // CONV_ROWS):
        rows = slice(r * CONV_ROWS, (r + 1) * CONV_ROWS)
        init = jnp.zeros((CONV_ROWS, width), F32)
        y = _causal_taps(ubuf, cw_ref, SMALL_HALO, r * CONV_ROWS, CONV_ROWS, init)
        o_ref[rows, :] = (bbuf[rows, :] * y).astype(o_ref.dtype)


def _shortconv(xn, w, conv_w):
    width = BRANCH_WIDTH
    scratch = lambda tile: [pltpu.VMEM((SMALL_HALO + tile, width), F32), pltpu.VMEM((tile, width), F32)]
    return _seq_call(_sc_body, xn, [w, conv_w], width, scratch, "shortconv")


def _gmlp_body(xn_ref, w_ref, lw_ref, lb_ref, ws_ref, bs_ref, o_ref, ubuf, vbuf):
    tile, width = o_ref.shape
    gc = width // GMLP_GROUPS
    h = jnp.dot(xn_ref[...], w_ref[...], preferred_element_type=F32)
    g = jax.nn.gelu(h)
    ubuf[...] = g[:, :width]
    vbuf[...] = _layernorm(g[:, width:], lw_ref[...], lb_ref[...]).astype(BF16)
    row = lax.broadcasted_iota(I32, (GMLP_CHUNK, GMLP_CHUNK), 0)
    col = lax.broadcasted_iota(I32, (GMLP_CHUNK, GMLP_CHUNK), 1)
    for gi in range(GMLP_GROUPS):
        w_c = jnp.where(row >= col, ws_ref[gi], 0.0).astype(BF16)
        cols = slice(gi * gc, (gi + 1) * gc)
        for c in range(tile // GMLP_CHUNK):
            rows = slice(c * GMLP_CHUNK, (c + 1) * GMLP_CHUNK)
            mixed = jnp.dot(w_c, vbuf[rows, cols], preferred_element_type=F32) + bs_ref[gi]
            o_ref[rows, cols] = (ubuf[rows, cols] * mixed).astype(o_ref.dtype)


def _gmlp(xn, w, ln_w, ln_b, w_s, b_s):
    width = BRANCH_WIDTH
    gc = width // GMLP_GROUPS
    b_full = jnp.broadcast_to(b_s[:, :, None], (GMLP_GROUPS, GMLP_CHUNK, gc))
    weights = [w, ln_w.reshape(1, width), ln_b.reshape(1, width), w_s, b_full]
    scratch = lambda tile: [pltpu.VMEM((tile, width), F32), pltpu.VMEM((tile, width), BF16)]
    return _seq_call(_gmlp_body, xn, weights, width, scratch, "gmlp")


def _softplus(x):
    return jnp.maximum(x, 0.0) + jnp.log(1.0 + jnp.exp(-jnp.abs(x)))


def _bmm(a, b):
    return jnp.einsum("nij,njk->nik", a.astype(BF16), b.astype(BF16), preferred_element_type=F32)


def _bmm_nt(a, b):
    return jnp.einsum("nid,njd->nij", a.astype(BF16), b.astype(BF16), preferred_element_type=F32)


def _bmm_tn(a, b):
    return jnp.einsum("ncd,nce->nde", a.astype(BF16), b.astype(BF16), preferred_element_type=F32)


def _unit_lower_inverse(strict_lower, eye):
    size = strict_lower.shape[-1]
    inv = eye - strict_lower
    power = strict_lower
    span = 2
    while span < size:
        power = _bmm(power, power)
        inv = inv + _bmm(inv, power)
        span *= 2
    return inv


def _split3(x):
    hi = x.astype(BF16)
    r1 = x - hi.astype(F32)
    mid = r1.astype(BF16)
    lo = (r1 - mid.astype(F32)).astype(BF16)
    return hi, mid, lo


def _gdn_body(xn_ref, w_ref, wb_ref, wa_ref, cw_ref, alog_ref, dtb_ref, nw_ref, o_ref,
              hbuf, qbuf, kbuf, vbuf, zbuf, gbuf, bbuf, s_ref, wc_ref, bc_ref, qp_ref, op_ref, egl_ref):
    tile = o_ref.shape[0]
    chunk = GDN_CHUNK
    t = pl.program_id(1)
    _carry_halo(hbuf, SMALL_HALO, tile)

    @pl.when(t == 0)
    def _():
        s_ref[...] = jnp.zeros(s_ref.shape, F32)

    xn = xn_ref[...]
    h = jnp.dot(xn, w_ref[...], preferred_element_type=F32)
    n_conv = 2 * GDN_QK + GDN_V
    hbuf[SMALL_HALO:SMALL_HALO + tile, :] = h[:, :n_conv]
    zbuf[...] = h[:, n_conv:]
    bbuf[...] = _sigmoid(jnp.dot(xn, wb_ref[...], preferred_element_type=F32))
    a_raw = jnp.dot(xn, wa_ref[...], preferred_element_type=F32)
    gbuf[...] = -jnp.exp(alog_ref[...]) * _softplus(a_raw + dtb_ref[...])

    for r in range(tile // CONV_ROWS):
        rows = slice(r * CONV_ROWS, (r + 1) * CONV_ROWS)
        init = jnp.zeros((CONV_ROWS, n_conv), F32)
        qkv = _silu(_causal_taps(hbuf, cw_ref, SMALL_HALO, r * CONV_ROWS, CONV_ROWS, init))
        for hd in range(GDN_HEADS):
            q = qkv[:, hd * GDN_DK:(hd + 1) * GDN_DK]
            k = qkv[:, GDN_QK + hd * GDN_DK:GDN_QK + (hd + 1) * GDN_DK]
            q = q * lax.rsqrt(jnp.sum(q * q, -1, keepdims=True) + EPS) * (GDN_DK ** -0.5)
            k = k * lax.rsqrt(jnp.sum(k * k, -1, keepdims=True) + EPS)
            qbuf[rows, hd * GDN_DK:(hd + 1) * GDN_DK] = q
            kbuf[rows, hd * GDN_DK:(hd + 1) * GDN_DK] = k
        vbuf[rows, :] = qkv[:, 2 * GDN_QK:]

    n_chunks = tile // chunk
    n_batch = GDN_HEADS * n_chunks
    row = lax.broadcasted_iota(I32, (chunk, chunk), 0)
    col = lax.broadcasted_iota(I32, (chunk, chunk), 1)
    tril = row >= col
    strict = row > col
    eye = jnp.where(row == col, 1.0, 0.0)

    def per_head(buf, width):
        return jnp.concatenate(
            [buf[:, hd * width:(hd + 1) * width].reshape(n_chunks, chunk, width)
             for hd in range(GDN_HEADS)], axis=0)

    g_lo, g_mid, g_hi = reversed(_split3(gbuf[...].reshape(n_chunks, chunk, LANES)))
    ones_tril = jnp.broadcast_to(jnp.where(tril, 1.0, 0.0).astype(BF16), (n_chunks, chunk, chunk))
    g_cum = _bmm(ones_tril, g_lo) + _bmm(ones_tril, g_mid) + _bmm(ones_tril, g_hi)
    g_cum_t = [g_cum[c].T for c in range(n_chunks)]
    g_col = jnp.concatenate([g_cum[:, :, hd:hd + 1] for hd in range(GDN_HEADS)], axis=0)
    g_row = jnp.concatenate(
        [g_cum_t[c][hd:hd + 1, :][None] for hd in range(GDN_HEADS) for c in range(n_chunks)], axis=0)
    b3 = bbuf[...].reshape(n_chunks, chunk, LANES)
    b_col = jnp.concatenate([b3[:, :, hd:hd + 1] for hd in range(GDN_HEADS)], axis=0)
    g_last = g_col[:, chunk - 1:chunk, :]

    q = per_head(qbuf[...], GDN_DK)
    k = per_head(kbuf[...], GDN_DK)
    v = per_head(vbuf[...], GDN_DV)
    decay = jnp.where(tril, jnp.exp(jnp.where(tril, g_col - g_row, 0.0)), 0.0)
    e_g = jnp.exp(g_col)
    k_beta = k * b_col
    inv = _unit_lower_inverse(jnp.where(strict, _bmm_nt(k_beta, k) * decay, 0.0), eye)
    uw = _bmm(inv, jnp.concatenate([v * b_col, k_beta * e_g], axis=2))
    u = uw[:, :, :GDN_DV]
    w = uw[:, :, GDN_DV:]
    attn = _bmm_nt(q, k) * decay
    k_dec = k * jnp.exp(g_last - g_col)
    wc_ref[...] = _bmm_tn(k_dec, w).astype(BF16)
    bc_ref[...] = _bmm_tn(k_dec, u)
    qp_ref[...] = (q * e_g - _bmm(attn, w)).astype(BF16)
    op_ref[...] = _bmm(attn, u)
    egl_ref[...] = jnp.broadcast_to(jnp.exp(g_last), (n_batch, 1, GDN_DV))

    for c in range(n_chunks):
        rows = slice(c * chunk, (c + 1) * chunk)
        for hd in range(GDN_HEADS):
            n = hd * n_chunks + c
            cols = slice(hd * GDN_DV, (hd + 1) * GDN_DV)
            state = s_ref[hd]
            s16 = state.astype(BF16)
            o = jnp.dot(qp_ref[n], s16, preferred_element_type=F32) + op_ref[n]
            s_ref[hd] = (state * egl_ref[n] + bc_ref[n]
                         - jnp.dot(wc_ref[n], s16, preferred_element_type=F32))
            o = o * lax.rsqrt(jnp.mean(o * o, -1, keepdims=True) + EPS) * nw_ref[...]
            o_ref[rows, cols] = (o * _silu(zbuf[rows, cols])).astype(o_ref.dtype)


def _lane_pad(vec):
    return jnp.zeros((1, LANES), F32).at[0, :vec.shape[0]].set(vec)


def _gdn(xn, w, w_beta, w_a, conv_w, a_log, dt_bias, norm_w):
    pad = ((0, 0), (0, LANES - GDN_HEADS))
    weights = [w, jnp.pad(w_beta, pad), jnp.pad(w_a, pad), conv_w,
               _lane_pad(a_log), _lane_pad(dt_bias), norm_w.reshape(1, GDN_DV)]
    n_conv = 2 * GDN_QK + GDN_V

    def scratch(tile):
        n_batch = GDN_HEADS * (tile // GDN_CHUNK)
        return [pltpu.VMEM((SMALL_HALO + tile, n_conv), F32),
                pltpu.VMEM((tile, GDN_QK), F32), pltpu.VMEM((tile, GDN_QK), F32),
                pltpu.VMEM((tile, GDN_V), F32), pltpu.VMEM((tile, GDN_V), F32),
                pltpu.VMEM((tile, LANES), F32), pltpu.VMEM((tile, LANES), F32),
                pltpu.VMEM((GDN_HEADS, GDN_DK, GDN_DV), F32),
                pltpu.VMEM((n_batch, GDN_DK, GDN_DK), BF16),
                pltpu.VMEM((n_batch, GDN_DK, GDN_DV), F32),
                pltpu.VMEM((n_batch, GDN_CHUNK, GDN_DK), BF16),
                pltpu.VMEM((n_batch, GDN_CHUNK, GDN_DV), F32),
                pltpu.VMEM((n_batch, 1, GDN_DV), F32)]

    return _seq_call(_gdn_body, xn, weights, GDN_V, scratch, "gdn")


def _merge_body(x_ref, xn_ref, y0_ref, y1_ref, y2_ref, y3_ref, wg_ref, wb_ref, wo_ref, o_ref):
    xn = xn_ref[...]
    merged = None
    for i, y_ref in enumerate((y0_ref, y1_ref, y2_ref, y3_ref)):
        gate = _sigmoid(jnp.dot(xn, wg_ref[i], preferred_element_type=F32))
        term = gate * jnp.dot(y_ref[...], wb_ref[i], preferred_element_type=F32)
        merged = term if merged is None else merged + term
    o_ref[...] = x_ref[...] + jnp.dot(merged.astype(BF16), wo_ref[...], preferred_element_type=F32)


def _merge(x2d, xn2d, ys, w_gate, w_branch, w_out):
    n, d = x2d.shape
    tm = min(ROW_TILE, n)
    row = lambda width: pl.BlockSpec((tm, width), lambda i: (i, 0))
    return pl.pallas_call(
        _merge_body,
        grid=(n // tm,),
        in_specs=[row(d), row(d)] + [row(BRANCH_WIDTH)] * N_BRANCH
        + [_full(w_gate.shape), _full(w_branch.shape), _full(w_out.shape)],
        out_specs=row(d),
        out_shape=jax.ShapeDtypeStruct((n, d), F32),
        compiler_params=_params(("parallel",)),
        name="merge",
    )(x2d, xn2d, *ys, w_gate, w_branch, w_out)


def _split2(x):
    hi = x.astype(BF16)
    lo = (x - hi.astype(F32)).astype(BF16)
    return hi, lo


def _router_body(x_ref, nw_ref, wr_ref, xn_ref, info_ref, cnt_ref):
    tm = x_ref.shape[0]

    @pl.when(pl.program_id(0) == 0)
    def _():
        cnt_ref[...] = jnp.zeros(cnt_ref.shape, F32)

    x = x_ref[...]
    xn = x * lax.rsqrt(jnp.mean(x * x, -1, keepdims=True) + EPS) * nw_ref[...]
    xn_ref[...] = xn
    x_hi, x_lo = _split2(xn)
    logits = (jnp.dot(x_lo, wr_ref[0], preferred_element_type=F32)
              + jnp.dot(x_hi, wr_ref[1], preferred_element_type=F32)
              + jnp.dot(x_hi, wr_ref[0], preferred_element_type=F32))

    lane = lax.broadcasted_iota(I32, (tm, LANES), 1)
    neg = jnp.float32(-jnp.inf)
    big = jnp.int32(2 * LANES)
    is_grp = (lane >= GRP_LANE0) & (lane < GRP_LANE0 + MOE_GROUPS)
    lg = jnp.where(is_grp, logits, neg)
    mg = jnp.max(lg, -1, keepdims=True)
    grp = jnp.min(jnp.where(lg == mg, lane, big), -1, keepdims=True) - GRP_LANE0
    p_sel = 1.0 / jnp.sum(jnp.where(is_grp, jnp.exp(lg - mg), 0.0), -1, keepdims=True)

    in_grp = (lane < N_EXPERTS) & ((lane // MOE_PER_GROUP) == grp)
    le = jnp.where(in_grp, logits, neg)
    m0 = jnp.max(le, -1, keepdims=True)
    e0 = jnp.min(jnp.where(le == m0, lane, big), -1, keepdims=True)
    le1 = jnp.where(lane == e0, neg, le)
    m1 = jnp.max(le1, -1, keepdims=True)
    e1 = jnp.min(jnp.where(le1 == m1, lane, big), -1, keepdims=True)
    r = jnp.exp(m1 - m0)
    w0 = p_sel / (1.0 + r)
    w1 = p_sel * r / (1.0 + r)

    hot0 = lane == e0
    hot1 = lane == e1
    onehot = jnp.where(hot0 | hot1, 1.0, 0.0)
    ti = lax.broadcasted_iota(I32, (tm, tm), 0)
    tj = lax.broadcasted_iota(I32, (tm, tm), 1)
    before = jnp.where(ti > tj, 1.0, 0.0).astype(BF16)
    base = jnp.dot(before, onehot.astype(BF16), preferred_element_type=F32) + cnt_ref[...]
    r0 = jnp.sum(jnp.where(hot0, base, 0.0), -1, keepdims=True)
    r1 = jnp.sum(jnp.where(hot1, base, 0.0), -1, keepdims=True)
    cnt_ref[...] = cnt_ref[...] + jnp.sum(onehot, 0, keepdims=True)

    info = jnp.zeros((tm, LANES), F32)
    for idx, val in ((INFO_E0, e0.astype(F32)), (INFO_E1, e1.astype(F32)), (INFO_W0, w0),
                     (INFO_W1, w1), (INFO_R0, r0), (INFO_R1, r1)):
        info = jnp.where(lane == idx, val, info)
    info_ref[...] = info


def _router(x2d, norm_w, router_grp, router_exp):
    n, d = x2d.shape
    tm = min(ROW_TILE, n)
    w_r = jnp.zeros((d, LANES), F32).at[:, :N_EXPERTS].set(router_exp)
    w_r = w_r.at[:, GRP_LANE0:GRP_LANE0 + MOE_GROUPS].set(router_grp)
    w_parts = jnp.stack(_split2(w_r))
    return pl.pallas_call(
        _router_body,
        grid=(n // tm,),
        in_specs=[pl.BlockSpec((tm, d), lambda i: (i, 0)), _full((1, d)), _full(w_parts.shape)],
        out_specs=[pl.BlockSpec((tm, d), lambda i: (i, 0)),
                   pl.BlockSpec((tm, LANES), lambda i: (i, 0)),
                   _full((1, LANES))],
        out_shape=[jax.ShapeDtypeStruct((n, d), F32),
                   jax.ShapeDtypeStruct((n, LANES), F32),
                   jax.ShapeDtypeStruct((1, LANES), F32)],
        compiler_params=_params(("arbitrary",)),
        name="router",
    )(x2d, norm_w.reshape(1, d), w_parts)


def _row_copy(src, src_row, dst, dst_row, sem):
    return pltpu.make_async_copy(src.at[pl.ds(src_row, 1)], dst.at[pl.ds(dst_row, 1)], sem)


def _dispatch_body(dest_ref, x_ref, xs_in_ref, xs_ref, sem):
    del xs_in_ref
    tm = x_ref.shape[0]

    def issue(r, carry):
        _row_copy(x_ref, r, xs_ref, dest_ref[2 * r], sem).start()
        _row_copy(x_ref, r, xs_ref, dest_ref[2 * r + 1], sem).start()
        return carry

    def drain(r, carry):
        _row_copy(x_ref, r, xs_ref, dest_ref[2 * r], sem).wait()
        _row_copy(x_ref, r, xs_ref, dest_ref[2 * r + 1], sem).wait()
        return carry

    lax.fori_loop(0, tm, issue, 0)
    lax.fori_loop(0, tm, drain, 0)


def _dispatch(xn2d, dest, n_rows):
    n, d = xn2d.shape
    tm = min(ROW_TILE, n)
    zeros = jnp.zeros((n_rows, d), xn2d.dtype)
    return pl.pallas_call(
        _dispatch_body,
        grid=(n // tm,),
        in_specs=[pl.BlockSpec((2 * tm,), lambda i: (i,), memory_space=pltpu.SMEM),
                  pl.BlockSpec((tm, d), lambda i: (i, 0)),
                  pl.BlockSpec(memory_space=pl.ANY)],
        out_specs=pl.BlockSpec(memory_space=pl.ANY),
        out_shape=jax.ShapeDtypeStruct((n_rows, d), xn2d.dtype),
        scratch_shapes=[pltpu.SemaphoreType.DMA(())],
        input_output_aliases={2: 0},
        compiler_params=_params(("arbitrary",)),
        name="dispatch",
    )(dest, xn2d, zeros)


def _ffn_body(blk_ref, xs_ref, wg_ref, wu_ref, wd_ref, y_ref, wg16, wu16, wd16):
    b = pl.program_id(0)
    prev = blk_ref[jnp.maximum(b - 1, 0)]

    @pl.when((b == 0) | (blk_ref[b] != prev))
    def _():
        wg16[...] = wg_ref[...].astype(BF16)
        wu16[...] = wu_ref[...].astype(BF16)
        wd16[...] = wd_ref[...].astype(BF16)

    xb = xs_ref[...].astype(BF16)
    g = jnp.dot(xb, wg16[...], preferred_element_type=F32)
    u = jnp.dot(xb, wu16[...], preferred_element_type=F32)
    hb = (_silu(g) * u).astype(BF16)
    y_ref[...] = jnp.dot(hb, wd16[...], preferred_element_type=F32)


def _ffn(xs, blk_expert, layer, w_gate, w_up, w_down):
    n_rows, d = xs.shape
    n_blocks = n_rows // MOE_BLOCK
    de = w_gate.shape[-1]
    grid_spec = pltpu.PrefetchScalarGridSpec(
        num_scalar_prefetch=1,
        grid=(n_blocks,),
        in_specs=[pl.BlockSpec((MOE_BLOCK, d), lambda b, e: (b, 0)),
                  pl.BlockSpec((None, None, d, de), lambda b, e: (layer, e[b], 0, 0)),
                  pl.BlockSpec((None, None, d, de), lambda b, e: (layer, e[b], 0, 0)),
                  pl.BlockSpec((None, None, de, d), lambda b, e: (layer, e[b], 0, 0))],
        out_specs=pl.BlockSpec((MOE_BLOCK, d), lambda b, e: (b, 0)),
        scratch_shapes=[pltpu.VMEM((d, de), BF16), pltpu.VMEM((d, de), BF16),
                        pltpu.VMEM((de, d), BF16)],
    )
    return pl.pallas_call(
        _ffn_body,
        grid_spec=grid_spec,
        out_shape=jax.ShapeDtypeStruct((n_rows, d), F32),
        compiler_params=_params(("arbitrary",)),
        name="expert_ffn",
    )(blk_expert, xs, w_gate, w_up, w_down)


def _combine_body(dest_ref, x_ref, info_ref, y_ref, o_ref, gbuf, sem):
    tm = x_ref.shape[0]

    def issue(r, carry):
        _row_copy(y_ref, dest_ref[2 * r], gbuf.at[0], r, sem).start()
        _row_copy(y_ref, dest_ref[2 * r + 1], gbuf.at[1], r, sem).start()
        return carry

    def drain(r, carry):
        _row_copy(y_ref, dest_ref[2 * r], gbuf.at[0], r, sem).wait()
        _row_copy(y_ref, dest_ref[2 * r + 1], gbuf.at[1], r, sem).wait()
        return carry

    lax.fori_loop(0, tm, issue, 0)
    lax.fori_loop(0, tm, drain, 0)
    info = info_ref[...]
    w0 = info[:, INFO_W0:INFO_W0 + 1]
    w1 = info[:, INFO_W1:INFO_W1 + 1]
    o_ref[...] = x_ref[...] + w0 * gbuf[0] + w1 * gbuf[1]


def _combine(x2d, info, dest, y):
    n, d = x2d.shape
    tm = min(ROW_TILE, n)
    return pl.pallas_call(
        _combine_body,
        grid=(n // tm,),
        in_specs=[pl.BlockSpec((2 * tm,), lambda i: (i,), memory_space=pltpu.SMEM),
                  pl.BlockSpec((tm, d), lambda i: (i, 0)),
                  pl.BlockSpec((tm, LANES), lambda i: (i, 0)),
                  pl.BlockSpec(memory_space=pl.ANY)],
        out_specs=pl.BlockSpec((tm, d), lambda i: (i, 0)),
        out_shape=jax.ShapeDtypeStruct((n, d), F32),
        scratch_shapes=[pltpu.VMEM((2, tm, d), F32), pltpu.SemaphoreType.DMA(())],
        compiler_params=_params(("arbitrary",)),
        name="combine",
    )(dest, x2d, info, y)


def _hier_moe(x2d, norm_w, router_grp, router_exp, layer, w_gate, w_up, w_down):
    n, _ = x2d.shape
    xn, info, cnt = _router(x2d, norm_w, router_grp, router_exp)
    counts = cnt[0, :N_EXPERTS].astype(I32)
    padded = (counts + MOE_BLOCK - 1) // MOE_BLOCK * MOE_BLOCK
    pad_end = jnp.cumsum(padded)
    pad_start = pad_end - padded
    expert = info[:, INFO_E0:INFO_E1 + 1].astype(I32)
    rank = info[:, INFO_R0:INFO_R1 + 1].astype(I32)
    dest = (pad_start[expert] + rank).reshape(-1)
    n_blocks = (2 * n + N_EXPERTS * (MOE_BLOCK - 1)) // MOE_BLOCK + 1
    blk_start = jnp.arange(n_blocks, dtype=I32) * MOE_BLOCK
    blk_expert = jnp.minimum(jnp.sum((pad_end[None, :] <= blk_start[:, None]).astype(I32), axis=1),
                             N_EXPERTS - 1)
    xs = _dispatch(xn, dest, n_blocks * MOE_BLOCK)
    y = _ffn(xs, blk_expert, layer, w_gate, w_up, w_down)
    return _combine(x2d, info, dest, y)


def _mixer_block(x, norm_w, w_in, gdn_conv_w, gdn_a_log, gdn_dt_bias, gdn_norm_w,
                 conf_conv_w, conf_conv_b, conf_ln_w, conf_ln_b, gmlp_ln_w, gmlp_ln_b,
                 gmlp_w_s, gmlp_b_s, sc_conv_w, w_branch, w_out):
    bsz, t_len, d = x.shape
    x2d = x.reshape(bsz * t_len, d)
    xn2d = _rmsnorm(x2d, norm_w, BF16)
    xn = xn2d.reshape(bsz, t_len, d)
    c0 = 2 * GDN_QK + 2 * GDN_V
    c1 = c0 + GDN_HEADS
    c2 = c1 + GDN_HEADS
    c3 = c2 + 2 * BRANCH_WIDTH
    c4 = c3 + 2 * BRANCH_WIDTH
    c5 = c4 + 3 * BRANCH_WIDTH
    w16 = w_in.astype(BF16)
    ys = (
        _gdn(xn, w16[:, :c0], w16[:, c0:c1], w16[:, c1:c2], gdn_conv_w, gdn_a_log, gdn_dt_bias,
             gdn_norm_w),
        _conformer(xn, w16[:, c2:c3], conf_conv_w, conf_conv_b, conf_ln_w, conf_ln_b),
        _gmlp(xn, w16[:, c3:c4], gmlp_ln_w, gmlp_ln_b, gmlp_w_s, gmlp_b_s),
        _shortconv(xn, w16[:, c4:c5], sc_conv_w),
    )
    ys = [y.reshape(bsz * t_len, BRANCH_WIDTH) for y in ys]
    w_gate = w16[:, c5:].reshape(d, N_BRANCH, d).transpose(1, 0, 2)
    return _merge(x2d, xn2d, ys, w_gate, w_branch.astype(BF16), w_out.astype(BF16))


def kernel(x, mix_norm_w, w_in, gdn_conv_w, gdn_A_log, gdn_dt_bias, gdn_norm_w, conf_conv_w,
           conf_conv_b, conf_ln_w, conf_ln_b, gmlp_ln_w, gmlp_ln_b, gmlp_w_s, gmlp_b_s, sc_conv_w,
           w_branch, w_out, moe_norm_w, router_grp, router_exp, w_gate, w_up, w_down, final_norm_w):
    bsz, t_len, d = x.shape
    for l in range(mix_norm_w.shape[0]):
        x2d = _mixer_block(x, mix_norm_w[l], w_in[l], gdn_conv_w[l], gdn_A_log[l], gdn_dt_bias[l],
                           gdn_norm_w[l], conf_conv_w[l], conf_conv_b[l], conf_ln_w[l],
                           conf_ln_b[l], gmlp_ln_w[l], gmlp_ln_b[l], gmlp_w_s[l], gmlp_b_s[l],
                           sc_conv_w[l], w_branch[l], w_out[l])
        x2d = _hier_moe(x2d, moe_norm_w[l], router_grp[l], router_exp[l], l, w_gate, w_up, w_down)
        x = x2d.reshape(bsz, t_len, d)
    return _rmsnorm(x.reshape(bsz * t_len, d), final_norm_w, F32).reshape(bsz, t_len, d)
```

```python
import functools

import jax
import jax.numpy as jnp
from jax import lax
from jax.experimental import pallas as pl
from jax.experimental.pallas import tpu as pltpu

F32 = jnp.float32
BF16 = jnp.bfloat16
I32 = jnp.int32

D_MODEL = 1024
BRANCH_WIDTH = 512
N_BRANCH = 4
GDN_HEADS = 4
GDN_DK = 128
GDN_DV = 128
GDN_CONV = 4
GDN_CHUNK = 64
GDN_QK = GDN_HEADS * GDN_DK
GDN_V = GDN_HEADS * GDN_DV
CONF_KERNEL = 31
GMLP_GROUPS = 4
GMLP_CHUNK = 128
SC_KERNEL = 3
MOE_GROUPS = 8
MOE_PER_GROUP = 8
N_EXPERTS = MOE_GROUPS * MOE_PER_GROUP
D_EXPERT = 512
MOE_BLOCK = 256
EPS = 1e-6

LANES = 128
SUBLANES = 8
VMEM_LIMIT = 56 * 1024 * 1024

SEQ_TILE = 512
ROW_TILE = 512
CONF_HALO = 32
SMALL_HALO = SUBLANES
CONV_ROWS = 64

INFO_E0, INFO_E1, INFO_W0, INFO_W1, INFO_R0, INFO_R1 = range(6)
GRP_LANE0 = N_EXPERTS


def _params(sem):
    return pltpu.CompilerParams(dimension_semantics=sem, vmem_limit_bytes=VMEM_LIMIT)


def _full(shape):
    nd = len(shape)
    return pl.BlockSpec(shape, lambda *_: (0,) * nd)


def _bdot(a, b):
    return jnp.dot(a.astype(BF16), b.astype(BF16), preferred_element_type=F32)


def _sigmoid(x):
    return 1.0 / (1.0 + jnp.exp(-x))


def _silu(x):
    return x * _sigmoid(x)


def _rms_body(x_ref, w_ref, o_ref):
    x = x_ref[...]
    ms = jnp.mean(x * x, -1, keepdims=True)
    o_ref[...] = (x * lax.rsqrt(ms + EPS) * w_ref[...]).astype(o_ref.dtype)


def _rmsnorm(x2d, w, out_dtype):
    n, d = x2d.shape
    tm = min(ROW_TILE, n)
    return pl.pallas_call(
        _rms_body,
        grid=(n // tm,),
        in_specs=[pl.BlockSpec((tm, d), lambda i: (i, 0)), _full((1, d))],
        out_specs=pl.BlockSpec((tm, d), lambda i: (i, 0)),
        out_shape=jax.ShapeDtypeStruct((n, d), out_dtype),
        compiler_params=_params(("parallel",)),
        name="rmsnorm",
    )(x2d, w.reshape(1, d))


def _carry_halo(buf, halo, tile):
    t = pl.program_id(1)

    @pl.when(t == 0)
    def _():
        buf[0:halo, :] = jnp.zeros((halo, buf.shape[1]), buf.dtype)

    @pl.when(t != 0)
    def _():
        buf[0:halo, :] = buf[tile:tile + halo, :]


def _causal_taps(buf, cw_ref, halo, row0, rows, init):
    k_w = cw_ref.shape[0]
    acc = init
    for k in range(k_w):
        off = halo - (k_w - 1) + k + row0
        acc = acc + cw_ref[k:k + 1, :] * buf[off:off + rows, :]
    return acc


def _causal_taps_by_phase(buf, cw_ref, halo, row0, rows, init):
    k_w = cw_ref.shape[0]
    base = halo - (k_w - 1)
    acc = init
    for phase in range(SUBLANES):
        taps = [k for k in range(k_w) if (base + k) % SUBLANES == phase]
        span = rows + (SUBLANES if phase else 0)
        part = None
        for k in taps:
            off = base + k - phase + row0
            term = cw_ref[k:k + 1, :] * buf[off:off + span, :]
            part = term if part is None else part + term
        if part is not None:
            acc = acc + part[phase:phase + rows, :]
    return acc


def _layernorm(x, w, b):
    mu = jnp.mean(x, -1, keepdims=True)
    xc = x - mu
    var = jnp.mean(xc * xc, -1, keepdims=True)
    return xc * lax.rsqrt(var + EPS) * w + b


def _seq_call(body, xn, weights, out_width, scratch, name):
    bsz, t_len, d = xn.shape
    tile = min(SEQ_TILE, t_len)
    in_specs = [pl.BlockSpec((None, tile, d), lambda b, t: (b, t, 0))]
    in_specs += [_full(w.shape) for w in weights]
    return pl.pallas_call(
        body,
        grid=(bsz, t_len // tile),
        in_specs=in_specs,
        out_specs=pl.BlockSpec((None, tile, out_width), lambda b, t: (b, t, 0)),
        out_shape=jax.ShapeDtypeStruct((bsz, t_len, out_width), BF16),
        scratch_shapes=scratch(tile),
        compiler_params=_params(("parallel", "arbitrary")),
        name=name,
    )(xn, *weights)


def _conf_body(xn_ref, w_ref, cw_ref, cb_ref, lw_ref, lb_ref, o_ref, ybuf):
    tile, width = o_ref.shape
    _carry_halo(ybuf, CONF_HALO, tile)
    h = jnp.dot(xn_ref[...], w_ref[...], preferred_element_type=F32)
    ybuf[CONF_HALO:CONF_HALO + tile, :] = h[:, :width] * _sigmoid(h[:, width:])
    for r in range(tile // CONV_ROWS):
        init = jnp.broadcast_to(cb_ref[...], (CONV_ROWS, width))
        y = _causal_taps_by_phase(ybuf, cw_ref, CONF_HALO, r * CONV_ROWS, CONV_ROWS, init)
        y = _layernorm(y, lw_ref[...], lb_ref[...])
        o_ref[r * CONV_ROWS:(r + 1) * CONV_ROWS, :] = _silu(y).astype(o_ref.dtype)


def _conformer(xn, w, conv_w, conv_b, ln_w, ln_b):
    width = BRANCH_WIDTH
    weights = [w, conv_w, conv_b.reshape(1, width), ln_w.reshape(1, width), ln_b.reshape(1, width)]
    scratch = lambda tile: [pltpu.VMEM((CONF_HALO + tile, width), F32)]
    return _seq_call(_conf_body, xn, weights, width, scratch, "conformer")


def _sc_body(xn_ref, w_ref, cw_ref, o_ref, ubuf, bbuf):
    tile, width = o_ref.shape
    _carry_halo(ubuf, SMALL_HALO, tile)
    h = jnp.dot(xn_ref[...], w_ref[...], preferred_element_type=F32)
    bbuf[...] = h[:, :width]
    ubuf[SMALL_HALO:SMALL_HALO + tile, :] = h[:, width:2 * width] * h[:, 2 * width:]
    for r in range(tile // CONV_ROWS):
        rows = slice(r * CONV_ROWS, (r + 1) * CONV_ROWS)
        init = jnp.zeros((CONV_ROWS, width), F32)
        y = _causal_taps(ubuf, cw_ref, SMALL_HALO, r * CONV_ROWS, CONV_ROWS, init)
        o_ref[rows, :] = (bbuf[rows, :] * y).astype(o_ref.dtype)


def _shortconv(xn, w, conv_w):
    width = BRANCH_WIDTH
    scratch = lambda tile: [pltpu.VMEM((SMALL_HALO + tile, width), F32), pltpu.VMEM((tile, width), F32)]
    return _seq_call(_sc_body, xn, [w, conv_w], width, scratch, "shortconv")


def _gmlp_body(xn_ref, w_ref, lw_ref, lb_ref, ws_ref, bs_ref, o_ref, ubuf, vbuf):
    tile, width = o_ref.shape
    gc = width // GMLP_GROUPS
    h = jnp.dot(xn_ref[...], w_ref[...], preferred_element_type=F32)
    g = jax.nn.gelu(h)
    ubuf[...] = g[:, :width]
    vbuf[...] = _layernorm(g[:, width:], lw_ref[...], lb_ref[...]).astype(BF16)
    row = lax.broadcasted_iota(I32, (GMLP_CHUNK, GMLP_CHUNK), 0)
    col = lax.broadcasted_iota(I32, (GMLP_CHUNK, GMLP_CHUNK), 1)
    for gi in range(GMLP_GROUPS):
        w_c = jnp.where(row >= col, ws_ref[gi], 0.0).astype(BF16)
        cols = slice(gi * gc, (gi + 1) * gc)
        for c in range(tile // GMLP_CHUNK):
            rows = slice(c * GMLP_CHUNK, (c + 1) * GMLP_CHUNK)
            mixed = jnp.dot(w_c, vbuf[rows, cols], preferred_element_type=F32) + bs_ref[gi]
            o_ref[rows, cols] = (ubuf[rows, cols] * mixed).astype(o_ref.dtype)


def _gmlp(xn, w, ln_w, ln_b, w_s, b_s):
    width = BRANCH_WIDTH
    gc = width // GMLP_GROUPS
    b_full = jnp.broadcast_to(b_s[:, :, None], (GMLP_GROUPS, GMLP_CHUNK, gc))
    weights = [w, ln_w.reshape(1, width), ln_b.reshape(1, width), w_s, b_full]
    scratch = lambda tile: [pltpu.VMEM((tile, width), F32), pltpu.VMEM((tile, width), BF16)]
    return _seq_call(_gmlp_body, xn, weights, width, scratch, "gmlp")


def _softplus(x):
    return jnp.maximum(x, 0.0) + jnp.log(1.0 + jnp.exp(-jnp.abs(x)))


def _bmm(a, b):
    return jnp.einsum("nij,njk->nik", a.astype(BF16), b.astype(BF16), preferred_element_type=F32)


def _bmm_nt(a, b):
    return jnp.einsum("nid,njd->nij", a.astype(BF16), b.astype(BF16), preferred_element_type=F32)


def _bmm_tn(a, b):
    return jnp.einsum("ncd,nce->nde", a.astype(BF16), b.astype(BF16), preferred_element_type=F32)


def _unit_lower_inverse(strict_lower, eye):
    size = strict_lower.shape[-1]
    inv = eye - strict_lower
    power = strict_lower
    span = 2
    while span < size:
        power = _bmm(power, power)
        inv = inv + _bmm(inv, power)
        span *= 2
    return inv


def _split3(x):
    hi = x.astype(BF16)
    r1 = x - hi.astype(F32)
    mid = r1.astype(BF16)
    lo = (r1 - mid.astype(F32)).astype(BF16)
    return hi, mid, lo


def _gdn_body(xn_ref, w_ref, wb_ref, wa_ref, cw_ref, alog_ref, dtb_ref, nw_ref, o_ref,
              hbuf, qbuf, kbuf, vbuf, zbuf, gbuf, bbuf, s_ref, wc_ref, bc_ref, qp_ref, op_ref, egl_ref):
    tile = o_ref.shape[0]
    chunk = GDN_CHUNK
    t = pl.program_id(1)
    _carry_halo(hbuf, SMALL_HALO, tile)

    @pl.when(t == 0)
    def _():
        s_ref[...] = jnp.zeros(s_ref.shape, F32)

    xn = xn_ref[...]
    h = jnp.dot(xn, w_ref[...], preferred_element_type=F32)
    n_conv = 2 * GDN_QK + GDN_V
    hbuf[SMALL_HALO:SMALL_HALO + tile, :] = h[:, :n_conv]
    zbuf[...] = h[:, n_conv:]
    bbuf[...] = _sigmoid(jnp.dot(xn, wb_ref[...], preferred_element_type=F32))
    a_raw = jnp.dot(xn, wa_ref[...], preferred_element_type=F32)
    gbuf[...] = -jnp.exp(alog_ref[...]) * _softplus(a_raw + dtb_ref[...])

    for r in range(tile // CONV_ROWS):
        rows = slice(r * CONV_ROWS, (r + 1) * CONV_ROWS)
        init = jnp.zeros((CONV_ROWS, n_conv), F32)
        qkv = _silu(_causal_taps(hbuf, cw_ref, SMALL_HALO, r * CONV_ROWS, CONV_ROWS, init))
        for hd in range(GDN_HEADS):
            q = qkv[:, hd * GDN_DK:(hd + 1) * GDN_DK]
            k = qkv[:, GDN_QK + hd * GDN_DK:GDN_QK + (hd + 1) * GDN_DK]
            q = q * lax.rsqrt(jnp.sum(q * q, -1, keepdims=True) + EPS) * (GDN_DK ** -0.5)
            k = k * lax.rsqrt(jnp.sum(k * k, -1, keepdims=True) + EPS)
            qbuf[rows, hd * GDN_DK:(hd + 1) * GDN_DK] = q
            kbuf[rows, hd * GDN_DK:(hd + 1) * GDN_DK] = k
        vbuf[rows, :] = qkv[:, 2 * GDN_QK:]

    n_chunks = tile // chunk
    n_batch = GDN_HEADS * n_chunks
    row = lax.broadcasted_iota(I32, (chunk, chunk), 0)
    col = lax.broadcasted_iota(I32, (chunk, chunk), 1)
    tril = row >= col
    strict = row > col
    eye = jnp.where(row == col, 1.0, 0.0)

    def per_head(buf, width):
        return jnp.concatenate(
            [buf[:, hd * width:(hd + 1) * width].reshape(n_chunks, chunk, width)
             for hd in range(GDN_HEADS)], axis=0)

    g_lo, g_mid, g_hi = reversed(_split3(gbuf[...].reshape(n_chunks, chunk, LANES)))
    ones_tril = jnp.broadcast_to(jnp.where(tril, 1.0, 0.0).astype(BF16), (n_chunks, chunk, chunk))
    g_cum = _bmm(ones_tril, g_lo) + _bmm(ones_tril, g_mid) + _bmm(ones_tril, g_hi)
    g_cum_t = [g_cum[c].T for c in range(n_chunks)]
    g_col = jnp.concatenate([g_cum[:, :, hd:hd + 1] for hd in range(GDN_HEADS)], axis=0)
    g_row = jnp.concatenate(
        [g_cum_t[c][hd:hd + 1, :][None] for hd in range(GDN_HEADS) for c in range(n_chunks)], axis=0)
    b3 = bbuf[...].reshape(n_chunks, chunk, LANES)
    b_col = jnp.concatenate([b3[:, :, hd:hd + 1] for hd in range(GDN_HEADS)], axis=0)
    g_last = g_col[:, chunk - 1:chunk, :]

    q = per_head(qbuf[...], GDN_DK)
    k = per_head(kbuf[...], GDN_DK)
    v = per_head(vbuf[...], GDN_DV)
    decay = jnp.where(tril, jnp.exp(jnp.where(tril, g_col - g_row, 0.0)), 0.0)
    e_g = jnp.exp(g_col)
    k_beta = k * b_col
    inv = _unit_lower_inverse(jnp.where(strict, _bmm_nt(k_beta, k) * decay, 0.0), eye)
    uw = _bmm(inv, jnp.concatenate([v * b_col, k_beta * e_g], axis=2))
    u = uw[:, :, :GDN_DV]
    w = uw[:, :, GDN_DV:]
    attn = _bmm_nt(q, k) * decay
    k_dec = k * jnp.exp(g_last - g_col)
    wc_ref[...] = _bmm_tn(k_dec, w).astype(BF16)
    bc_ref[...] = _bmm_tn(k_dec, u)
    qp_ref[...] = (q * e_g - _bmm(attn, w)).astype(BF16)
    op_ref[...] = _bmm(attn, u)
    egl_ref[...] = jnp.broadcast_to(jnp.exp(g_last), (n_batch, 1, GDN_DV))

    for c in range(n_chunks):
        rows = slice(c * chunk, (c + 1) * chunk)
        for hd in range(GDN_HEADS):
            n = hd * n_chunks + c
            cols = slice(hd * GDN_DV, (hd + 1) * GDN_DV)
            state = s_ref[hd]
            s16 = state.astype(BF16)
            o = jnp.dot(qp_ref[n], s16, preferred_element_type=F32) + op_ref[n]
            s_ref[hd] = (state * egl_ref[n] + bc_ref[n]
                         - jnp.dot(wc_ref[n], s16, preferred_element_type=F32))
            o = o * lax.rsqrt(jnp.mean(o * o, -1, keepdims=True) + EPS) * nw_ref[...]
            o_ref[rows, cols] = (o * _silu(zbuf[rows, cols])).astype(o_ref.dtype)


def _lane_pad(vec):
    return jnp.zeros((1, LANES), F32).at[0, :vec.shape[0]].set(vec)


def _gdn(xn, w, w_beta, w_a, conv_w, a_log, dt_bias, norm_w):
    pad = ((0, 0), (0, LANES - GDN_HEADS))
    weights = [w, jnp.pad(w_beta, pad), jnp.pad(w_a, pad), conv_w,
               _lane_pad(a_log), _lane_pad(dt_bias), norm_w.reshape(1, GDN_DV)]
    n_conv = 2 * GDN_QK + GDN_V

    def scratch(tile):
        n_batch = GDN_HEADS * (tile // GDN_CHUNK)
        return [pltpu.VMEM((SMALL_HALO + tile, n_conv), F32),
                pltpu.VMEM((tile, GDN_QK), F32), pltpu.VMEM((tile, GDN_QK), F32),
                pltpu.VMEM((tile, GDN_V), F32), pltpu.VMEM((tile, GDN_V), F32),
                pltpu.VMEM((tile, LANES), F32), pltpu.VMEM((tile, LANES), F32),
                pltpu.VMEM((GDN_HEADS, GDN_DK, GDN_DV), F32),
                pltpu.VMEM((n_batch, GDN_DK, GDN_DK), BF16),
                pltpu.VMEM((n_batch, GDN_DK, GDN_DV), F32),
                pltpu.VMEM((n_batch, GDN_CHUNK, GDN_DK), BF16),
                pltpu.VMEM((n_batch, GDN_CHUNK, GDN_DV), F32),
                pltpu.VMEM((n_batch, 1, GDN_DV), F32)]

    return _seq_call(_gdn_body, xn, weights, GDN_V, scratch, "gdn")


def _merge_body(x_ref, xn_ref, y0_ref, y1_ref, y2_ref, y3_ref, wg_ref, wb_ref, wo_ref, o_ref):
    xn = xn_ref[...]
    merged = None
    for i, y_ref in enumerate((y0_ref, y1_ref, y2_ref, y3_ref)):
        gate = _sigmoid(jnp.dot(xn, wg_ref[i], preferred_element_type=F32))
        term = gate * jnp.dot(y_ref[...], wb_ref[i], preferred_element_type=F32)
        merged = term if merged is None else merged + term
    o_ref[...] = x_ref[...] + jnp.dot(merged.astype(BF16), wo_ref[...], preferred_element_type=F32)


def _merge(x2d, xn2d, ys, w_gate, w_branch, w_out):
    n, d = x2d.shape
    tm = min(ROW_TILE, n)
    row = lambda width: pl.BlockSpec((tm, width), lambda i: (i, 0))
    return pl.pallas_call(
        _merge_body,
        grid=(n // tm,),
        in_specs=[row(d), row(d)] + [row(BRANCH_WIDTH)] * N_BRANCH
        + [_full(w_gate.shape), _full(w_branch.shape), _full(w_out.shape)],
        out_specs=row(d),
        out_shape=jax.ShapeDtypeStruct((n, d), F32),
        compiler_params=_params(("parallel",)),
        name="merge",
    )(x2d, xn2d, *ys, w_gate, w_branch, w_out)


def _split2(x):
    hi = x.astype(BF16)
    lo = (x - hi.astype(F32)).astype(BF16)
    return hi, lo


def _router_body(x_ref, nw_ref, wr_ref, xn_ref, info_ref, cnt_ref):
    tm = x_ref.shape[0]

    @pl.when(pl.program_id(0) == 0)
    def _():
        cnt_ref[...] = jnp.zeros(cnt_ref.shape, F32)

    x = x_ref[...]
    xn = x * lax.rsqrt(jnp.mean(x * x, -1, keepdims=True) + EPS) * nw_ref[...]
    xn_ref[...] = xn
    x_hi, x_lo = _split2(xn)
    logits = (jnp.dot(x_lo, wr_ref[0], preferred_element_type=F32)
              + jnp.dot(x_hi, wr_ref[1], preferred_element_type=F32)
              + jnp.dot(x_hi, wr_ref[0], preferred_element_type=F32))

    lane = lax.broadcasted_iota(I32, (tm, LANES), 1)
    neg = jnp.float32(-jnp.inf)
    big = jnp.int32(2 * LANES)
    is_grp = (lane >= GRP_LANE0) & (lane < GRP_LANE0 + MOE_GROUPS)
    lg = jnp.where(is_grp, logits, neg)
    mg = jnp.max(lg, -1, keepdims=True)
    grp = jnp.min(jnp.where(lg == mg, lane, big), -1, keepdims=True) - GRP_LANE0
    p_sel = 1.0 / jnp.sum(jnp.where(is_grp, jnp.exp(lg - mg), 0.0), -1, keepdims=True)

    in_grp = (lane < N_EXPERTS) & ((lane // MOE_PER_GROUP) == grp)
    le = jnp.where(in_grp, logits, neg)
    m0 = jnp.max(le, -1, keepdims=True)
    e0 = jnp.min(jnp.where(le == m0, lane, big), -1, keepdims=True)
    le1 = jnp.where(lane == e0, neg, le)
    m1 = jnp.max(le1, -1, keepdims=True)
    e1 = jnp.min(jnp.where(le1 == m1, lane, big), -1, keepdims=True)
    r = jnp.exp(m1 - m0)
    w0 = p_sel / (1.0 + r)
    w1 = p_sel * r / (1.0 + r)

    hot0 = lane == e0
    hot1 = lane == e1
    onehot = jnp.where(hot0 | hot1, 1.0, 0.0)
    ti = lax.broadcasted_iota(I32, (tm, tm), 0)
    tj = lax.broadcasted_iota(I32, (tm, tm), 1)
    before = jnp.where(ti > tj, 1.0, 0.0).astype(BF16)
    base = jnp.dot(before, onehot.astype(BF16), preferred_element_type=F32) + cnt_ref[...]
    r0 = jnp.sum(jnp.where(hot0, base, 0.0), -1, keepdims=True)
    r1 = jnp.sum(jnp.where(hot1, base, 0.0), -1, keepdims=True)
    cnt_ref[...] = cnt_ref[...] + jnp.sum(onehot, 0, keepdims=True)

    info = jnp.zeros((tm, LANES), F32)
    for idx, val in ((INFO_E0, e0.astype(F32)), (INFO_E1, e1.astype(F32)), (INFO_W0, w0),
                     (INFO_W1, w1), (INFO_R0, r0), (INFO_R1, r1)):
        info = jnp.where(lane == idx, val, info)
    info_ref[...] = info


def _router(x2d, norm_w, router_grp, router_exp):
    n, d = x2d.shape
    tm = min(ROW_TILE, n)
    w_r = jnp.zeros((d, LANES), F32).at[:, :N_EXPERTS].set(router_exp)
    w_r = w_r.at[:, GRP_LANE0:GRP_LANE0 + MOE_GROUPS].set(router_grp)
    w_parts = jnp.stack(_split2(w_r))
    return pl.pallas_call(
        _router_body,
        grid=(n // tm,),
        in_specs=[pl.BlockSpec((tm, d), lambda i: (i, 0)), _full((1, d)), _full(w_parts.shape)],
        out_specs=[pl.BlockSpec((tm, d), lambda i: (i, 0)),
                   pl.BlockSpec((tm, LANES), lambda i: (i, 0)),
                   _full((1, LANES))],
        out_shape=[jax.ShapeDtypeStruct((n, d), F32),
                   jax.ShapeDtypeStruct((n, LANES), F32),
                   jax.ShapeDtypeStruct((1, LANES), F32)],
        compiler_params=_params(("arbitrary",)),
        name="router",
    )(x2d, norm_w.reshape(1, d), w_parts)


RANK_BITS = 16
SLOT_FREE = -1
INVERT_UNROLL = 8


INVERT_STEP = 1024


def _invert_body(code_ref, start_ref, free_hbm, slot_ref, sem):
    step = pl.program_id(0)

    @pl.when(step == 0)
    def _():
        fill = pltpu.make_async_copy(free_hbm, slot_ref, sem)
        fill.start()
        fill.wait()

    def body(i, carry):
        code = code_ref[i]
        expert = lax.shift_right_logical(code, RANK_BITS)
        rank = code & ((1 << RANK_BITS) - 1)
        slot_ref[start_ref[expert] + rank] = step * INVERT_STEP + i
        return carry

    lax.fori_loop(0, INVERT_STEP, body, 0, unroll=INVERT_UNROLL)


def _invert(code, pad_start, n_rows):
    smem = pl.BlockSpec(memory_space=pltpu.SMEM)
    return pl.pallas_call(
        _invert_body,
        grid=(code.shape[0] // INVERT_STEP,),
        in_specs=[pl.BlockSpec((INVERT_STEP,), lambda i: (i,), memory_space=pltpu.SMEM), smem,
                  pl.BlockSpec(memory_space=pl.ANY)],
        out_specs=smem,
        out_shape=jax.ShapeDtypeStruct((n_rows,), I32),
        scratch_shapes=[pltpu.SemaphoreType.DMA(())],
        compiler_params=_params(("arbitrary",)),
        name="invert",
    )(code, pad_start, jnp.full((n_rows,), SLOT_FREE, I32))


def _row_copies(idx_ref, hbm, buf, sem, to_vmem):
    copies = []
    for r in range(MOE_BLOCK):
        hbm_row = hbm.at[pl.ds(idx_ref[0, r], 1)]
        buf_row = buf.at[pl.ds(r, 1)]
        src, dst = (hbm_row, buf_row) if to_vmem else (buf_row, hbm_row)
        copies.append(pltpu.make_async_copy(src, dst, sem))
    return copies


def _ffn_body(blk_ref, last_ref, gcur_ref, gnext_ref, sprev_ref, scur_ref, x_hbm,
              wg_ref, wu_ref, wd_ref, yt_hbm, xbuf, ybuf, wg16, wu16, wd16, gsem, ssem):
    b = pl.program_id(0)
    last = last_ref[0]

    @pl.when(b <= last)
    def _():
        slot = b % 2
        other = 1 - slot

        @pl.when(b == 0)
        def _():
            ybuf[...] = jnp.zeros(ybuf.shape, F32)
            for cp in _row_copies(gcur_ref, x_hbm, xbuf.at[0], gsem.at[0], True):
                cp.start()

        @pl.when(b > 0)
        def _():
            for cp in _row_copies(sprev_ref, yt_hbm, ybuf.at[slot], ssem.at[slot], False):
                cp.wait()

        for cp in _row_copies(gcur_ref, x_hbm, xbuf.at[slot], gsem.at[slot], True):
            cp.wait()

        @pl.when((b == 0) | (blk_ref[b] != blk_ref[jnp.maximum(b - 1, 0)]))
        def _():
            wg16[...] = wg_ref[...].astype(BF16)
            wu16[...] = wu_ref[...].astype(BF16)
            wd16[...] = wd_ref[...].astype(BF16)

        for cp in _row_copies(gnext_ref, x_hbm, xbuf.at[other], gsem.at[other], True):
            cp.start()
        for cp in _row_copies(scur_ref, yt_hbm, ybuf.at[other], ssem.at[other], False):
            cp.start()
        xb = xbuf[slot].astype(BF16)
        g = jnp.dot(xb, wg16[...], preferred_element_type=F32)
        u = jnp.dot(xb, wu16[...], preferred_element_type=F32)
        hb = (_silu(g) * u).astype(BF16)
        ybuf[slot] = jnp.dot(hb, wd16[...], preferred_element_type=F32)

        @pl.when(b == last)
        def _():
            for cp in _row_copies(gnext_ref, x_hbm, xbuf.at[other], gsem.at[other], True):
                cp.wait()
            for cp in _row_copies(scur_ref, yt_hbm, ybuf.at[other], ssem.at[other], False):
                cp.wait()


def _ffn(xn2d, gather_rows, scatter_rows, blk_expert, last_step, n_out_rows, layer,
         w_gate, w_up, w_down):
    _, d = xn2d.shape
    de = w_gate.shape[-1]
    n_steps = blk_expert.shape[0]
    idx_spec = lambda shift: pl.BlockSpec(
        (None, 1, MOE_BLOCK), lambda b, e, u: (jnp.maximum(b + shift, 0), 0, 0),
        memory_space=pltpu.SMEM)
    w_spec = lambda rows, cols: pl.BlockSpec((None, None, rows, cols),
                                             lambda b, e, u: (layer, e[b], 0, 0))
    grid_spec = pltpu.PrefetchScalarGridSpec(
        num_scalar_prefetch=2,
        grid=(n_steps,),
        in_specs=[idx_spec(0), idx_spec(1), idx_spec(-1), idx_spec(0),
                  pl.BlockSpec(memory_space=pl.ANY),
                  w_spec(d, de), w_spec(d, de), w_spec(de, d)],
        out_specs=pl.BlockSpec(memory_space=pl.ANY),
        scratch_shapes=[pltpu.VMEM((2, MOE_BLOCK, d), F32), pltpu.VMEM((2, MOE_BLOCK, d), F32),
                        pltpu.VMEM((d, de), BF16), pltpu.VMEM((d, de), BF16),
                        pltpu.VMEM((de, d), BF16),
                        pltpu.SemaphoreType.DMA((2,)), pltpu.SemaphoreType.DMA((2,))],
    )
    return pl.pallas_call(
        _ffn_body,
        grid_spec=grid_spec,
        out_shape=jax.ShapeDtypeStruct((n_out_rows, d), F32),
        compiler_params=_params(("arbitrary",)),
        name="expert_ffn",
    )(blk_expert, last_step, gather_rows, gather_rows, scatter_rows, scatter_rows, xn2d,
      w_gate, w_up, w_down)


def _combine_body(x_ref, yt_ref, info_ref, nw_ref, *out_refs):
    d = x_ref.shape[1]
    info = info_ref[...]
    x = (x_ref[...] + info[:, INFO_W0:INFO_W0 + 1] * yt_ref[:, :d]
         + info[:, INFO_W1:INFO_W1 + 1] * yt_ref[:, d:])
    xn = x * lax.rsqrt(jnp.mean(x * x, -1, keepdims=True) + EPS) * nw_ref[...]
    if len(out_refs) == 2:
        out_refs[0][...] = x
    out_refs[-1][...] = xn.astype(out_refs[-1].dtype)


def _combine(x2d, yt, info, next_norm_w, final):
    n, d = x2d.shape
    tm = min(ROW_TILE, n)
    yt2 = yt.reshape(yt.shape[0] // 2, 2 * d)
    row = lambda width: pl.BlockSpec((tm, width), lambda i: (i, 0))
    out_shape = [jax.ShapeDtypeStruct((n, d), F32)]
    if not final:
        out_shape.append(jax.ShapeDtypeStruct((n, d), BF16))
    return pl.pallas_call(
        _combine_body,
        grid=(n // tm,),
        in_specs=[row(d), row(2 * d), row(LANES), _full((1, d))],
        out_specs=[row(d)] * len(out_shape),
        out_shape=out_shape,
        compiler_params=_params(("parallel",)),
        name="combine",
    )(x2d, yt2, info, next_norm_w.reshape(1, d))


def _hier_moe(x2d, norm_w, router_grp, router_exp, layer, w_gate, w_up, w_down, next_norm_w, final):
    n, _ = x2d.shape
    xn, info, cnt = _router(x2d, norm_w, router_grp, router_exp)
    counts = cnt[0, :N_EXPERTS].astype(I32)
    padded = (counts + MOE_BLOCK - 1) // MOE_BLOCK * MOE_BLOCK
    pad_end = jnp.cumsum(padded)
    pad_start = pad_end - padded
    expert = info[:, INFO_E0:INFO_E1 + 1].astype(I32)
    rank = info[:, INFO_R0:INFO_R1 + 1].astype(I32)
    code = ((expert << RANK_BITS) | rank).reshape(-1)
    n_blocks = (2 * n + N_EXPERTS * (MOE_BLOCK - 1)) // MOE_BLOCK + 1
    slots = _invert(code, pad_start, n_blocks * MOE_BLOCK)
    free = slots < 0
    spare = 2 * n + jnp.arange(MOE_BLOCK, dtype=I32)
    gather_rows = jnp.where(free, 0, slots >> 1)
    scatter_rows = jnp.where(free, jnp.tile(spare, n_blocks), slots)
    gather_rows = jnp.concatenate([gather_rows, jnp.zeros((2 * MOE_BLOCK,), I32)])
    scatter_rows = jnp.concatenate([spare, scatter_rows, spare])
    blk_start = jnp.arange(n_blocks + 1, dtype=I32) * MOE_BLOCK
    blk_expert = jnp.minimum(jnp.sum((pad_end[None, :] <= blk_start[:, None]).astype(I32), axis=1),
                             N_EXPERTS - 1)
    last_step = (pad_end[-1:] // MOE_BLOCK).astype(I32)
    yt = _ffn(xn, gather_rows.reshape(n_blocks + 2, 1, MOE_BLOCK),
              scatter_rows.reshape(n_blocks + 2, 1, MOE_BLOCK), blk_expert, last_step,
              2 * n + MOE_BLOCK, layer, w_gate, w_up, w_down)
    return _combine(x2d, yt, info, next_norm_w, final)


def _mixer_block(x2d, xn2d, bsz, w_in, gdn_conv_w, gdn_a_log, gdn_dt_bias, gdn_norm_w,
                 conf_conv_w, conf_conv_b, conf_ln_w, conf_ln_b, gmlp_ln_w, gmlp_ln_b,
                 gmlp_w_s, gmlp_b_s, sc_conv_w, w_branch, w_out):
    n, d = x2d.shape
    t_len = n // bsz
    xn = xn2d.reshape(bsz, t_len, d)
    c0 = 2 * GDN_QK + 2 * GDN_V
    c1 = c0 + GDN_HEADS
    c2 = c1 + GDN_HEADS
    c3 = c2 + 2 * BRANCH_WIDTH
    c4 = c3 + 2 * BRANCH_WIDTH
    c5 = c4 + 3 * BRANCH_WIDTH
    w16 = w_in.astype(BF16)
    ys = (
        _gdn(xn, w16[:, :c0], w16[:, c0:c1], w16[:, c1:c2], gdn_conv_w, gdn_a_log, gdn_dt_bias,
             gdn_norm_w),
        _conformer(xn, w16[:, c2:c3], conf_conv_w, conf_conv_b, conf_ln_w, conf_ln_b),
        _gmlp(xn, w16[:, c3:c4], gmlp_ln_w, gmlp_ln_b, gmlp_w_s, gmlp_b_s),
        _shortconv(xn, w16[:, c4:c5], sc_conv_w),
    )
    ys = [y.reshape(bsz * t_len, BRANCH_WIDTH) for y in ys]
    w_gate = w16[:, c5:].reshape(d, N_BRANCH, d).transpose(1, 0, 2)
    return _merge(x2d, xn2d, ys, w_gate, w_branch.astype(BF16), w_out.astype(BF16))


def kernel(x, mix_norm_w, w_in, gdn_conv_w, gdn_A_log, gdn_dt_bias, gdn_norm_w, conf_conv_w,
           conf_conv_b, conf_ln_w, conf_ln_b, gmlp_ln_w, gmlp_ln_b, gmlp_w_s, gmlp_b_s, sc_conv_w,
           w_branch, w_out, moe_norm_w, router_grp, router_exp, w_gate, w_up, w_down, final_norm_w):
    bsz, t_len, d = x.shape
    depth = mix_norm_w.shape[0]
    x2d = x.reshape(bsz * t_len, d)
    xn2d = _rmsnorm(x2d, mix_norm_w[0], BF16)
    for l in range(depth):
        x2d = _mixer_block(x2d, xn2d, bsz, w_in[l], gdn_conv_w[l], gdn_A_log[l], gdn_dt_bias[l],
                           gdn_norm_w[l], conf_conv_w[l], conf_conv_b[l], conf_ln_w[l],
                           conf_ln_b[l], gmlp_ln_w[l], gmlp_ln_b[l], gmlp_w_s[l], gmlp_b_s[l],
                           sc_conv_w[l], w_branch[l], w_out[l])
        final = l == depth - 1
        next_norm_w = final_norm_w if final else mix_norm_w[l + 1]
        outs = _hier_moe(x2d, moe_norm_w[l], router_grp[l], router_exp[l], l, w_gate, w_up, w_down,
                         next_norm_w, final)
        if final:
            return outs[0].reshape(bsz, t_len, d)
        x2d, xn2d = outs
```

```python
import functools

import jax
import jax.numpy as jnp
from jax import lax
from jax.experimental import pallas as pl
from jax.experimental.pallas import tpu as pltpu

F32 = jnp.float32
BF16 = jnp.bfloat16
I32 = jnp.int32

D_MODEL = 1024
BRANCH_WIDTH = 512
N_BRANCH = 4
GDN_HEADS = 4
GDN_DK = 128
GDN_DV = 128
GDN_CONV = 4
GDN_CHUNK = 64
GDN_QK = GDN_HEADS * GDN_DK
GDN_V = GDN_HEADS * GDN_DV
CONF_KERNEL = 31
GMLP_GROUPS = 4
GMLP_CHUNK = 128
SC_KERNEL = 3
MOE_GROUPS = 8
MOE_PER_GROUP = 8
N_EXPERTS = MOE_GROUPS * MOE_PER_GROUP
D_EXPERT = 512
MOE_BLOCK = 256
EPS = 1e-6

LANES = 128
SUBLANES = 8
VMEM_LIMIT = 56 * 1024 * 1024

SEQ_TILE = 512
ROW_TILE = 512
CONF_HALO = 32
SMALL_HALO = SUBLANES
CONV_ROWS = 64

INFO_E0, INFO_E1, INFO_W0, INFO_W1, INFO_R0, INFO_R1 = range(6)
GRP_LANE0 = N_EXPERTS


def _params(sem):
    return pltpu.CompilerParams(dimension_semantics=sem, vmem_limit_bytes=VMEM_LIMIT)


def _full(shape):
    nd = len(shape)
    return pl.BlockSpec(shape, lambda *_: (0,) * nd)


def _bdot(a, b):
    return jnp.dot(a.astype(BF16), b.astype(BF16), preferred_element_type=F32)


def _sigmoid(x):
    return 1.0 / (1.0 + jnp.exp(-x))


def _silu(x):
    return x * _sigmoid(x)


def _rms_body(x_ref, w_ref, o_ref):
    x = x_ref[...]
    ms = jnp.mean(x * x, -1, keepdims=True)
    o_ref[...] = (x * lax.rsqrt(ms + EPS) * w_ref[...]).astype(o_ref.dtype)


def _rmsnorm(x2d, w, out_dtype):
    n, d = x2d.shape
    tm = min(ROW_TILE, n)
    return pl.pallas_call(
        _rms_body,
        grid=(n // tm,),
        in_specs=[pl.BlockSpec((tm, d), lambda i: (i, 0)), _full((1, d))],
        out_specs=pl.BlockSpec((tm, d), lambda i: (i, 0)),
        out_shape=jax.ShapeDtypeStruct((n, d), out_dtype),
        compiler_params=_params(("parallel",)),
        name="rmsnorm",
    )(x2d, w.reshape(1, d))


def _carry_halo(buf, halo, tile):
    t = pl.program_id(1)

    @pl.when(t == 0)
    def _():
        buf[0:halo, :] = jnp.zeros((halo, buf.shape[1]), buf.dtype)

    @pl.when(t != 0)
    def _():
        buf[0:halo, :] = buf[tile:tile + halo, :]


def _causal_taps(buf, cw_ref, halo, row0, rows, init):
    k_w = cw_ref.shape[0]
    acc = init
    for k in range(k_w):
        off = halo - (k_w - 1) + k + row0
        acc = acc + cw_ref[k:k + 1, :] * buf[off:off + rows, :]
    return acc


def _causal_taps_by_phase(buf, cw_ref, halo, row0, rows, init):
    k_w = cw_ref.shape[0]
    base = halo - (k_w - 1)
    acc = init
    for phase in range(SUBLANES):
        taps = [k for k in range(k_w) if (base + k) % SUBLANES == phase]
        span = rows + (SUBLANES if phase else 0)
        part = None
        for k in taps:
            off = base + k - phase + row0
            term = cw_ref[k:k + 1, :] * buf[off:off + span, :]
            part = term if part is None else part + term
        if part is not None:
            acc = acc + part[phase:phase + rows, :]
    return acc


def _layernorm(x, w, b):
    mu = jnp.mean(x, -1, keepdims=True)
    xc = x - mu
    var = jnp.mean(xc * xc, -1, keepdims=True)
    return xc * lax.rsqrt(var + EPS) * w + b


def _seq_call(body, xn, weights, out_width, scratch, name):
    bsz, t_len, d = xn.shape
    tile = min(SEQ_TILE, t_len)
    in_specs = [pl.BlockSpec((None, tile, d), lambda b, t: (b, t, 0))]
    in_specs += [_full(w.shape) for w in weights]
    return pl.pallas_call(
        body,
        grid=(bsz, t_len // tile),
        in_specs=in_specs,
        out_specs=pl.BlockSpec((None, tile, out_width), lambda b, t: (b, t, 0)),
        out_shape=jax.ShapeDtypeStruct((bsz, t_len, out_width), BF16),
        scratch_shapes=scratch(tile),
        compiler_params=_params(("parallel", "arbitrary")),
        name=name,
    )(xn, *weights)


def _conf_body(xn_ref, w_ref, cw_ref, cb_ref, lw_ref, lb_ref, o_ref, ybuf):
    tile, width = o_ref.shape
    _carry_halo(ybuf, CONF_HALO, tile)
    h = jnp.dot(xn_ref[...], w_ref[...], preferred_element_type=F32)
    ybuf[CONF_HALO:CONF_HALO + tile, :] = h[:, :width] * _sigmoid(h[:, width:])
    for r in range(tile // CONV_ROWS):
        init = jnp.broadcast_to(cb_ref[...], (CONV_ROWS, width))
        y = _causal_taps_by_phase(ybuf, cw_ref, CONF_HALO, r * CONV_ROWS, CONV_ROWS, init)
        y = _layernorm(y, lw_ref[...], lb_ref[...])
        o_ref[r * CONV_ROWS:(r + 1) * CONV_ROWS, :] = _silu(y).astype(o_ref.dtype)


def _conformer(xn, w, conv_w, conv_b, ln_w, ln_b):
    width = BRANCH_WIDTH
    weights = [w, conv_w, conv_b.reshape(1, width), ln_w.reshape(1, width), ln_b.reshape(1, width)]
    scratch = lambda tile: [pltpu.VMEM((CONF_HALO + tile, width), F32)]
    return _seq_call(_conf_body, xn, weights, width, scratch, "conformer")


def _sc_body(xn_ref, w_ref, cw_ref, o_ref, ubuf, bbuf):
    tile, width = o_ref.shape
    _carry_halo(ubuf, SMALL_HALO, tile)
    h = jnp.dot(xn_ref[...], w_ref[...], preferred_element_type=F32)
    bbuf[...] = h[:, :width]
    ubuf[SMALL_HALO:SMALL_HALO + tile, :] = h[:, width:2 * width] * h[:, 2 * width:]
    for r in range(tile // CONV_ROWS):
        rows = slice(r * CONV_ROWS, (r + 1) * CONV_ROWS)
        init = jnp.zeros((CONV_ROWS, width), F32)
        y = _causal_taps(ubuf, cw_ref, SMALL_HALO, r * CONV_ROWS, CONV_ROWS, init)
        o_ref[rows, :] = (bbuf[rows, :] * y).astype(o_ref.dtype)


def _shortconv(xn, w, conv_w):
    width = BRANCH_WIDTH
    scratch = lambda tile: [pltpu.VMEM((SMALL_HALO + tile, width), F32), pltpu.VMEM((tile, width), F32)]
    return _seq_call(_sc_body, xn, [w, conv_w], width, scratch, "shortconv")


def _gmlp_body(xn_ref, w_ref, lw_ref, lb_ref, ws_ref, bs_ref, o_ref, ubuf, vbuf):
    tile, width = o_ref.shape
    gc = width // GMLP_GROUPS
    h = jnp.dot(xn_ref[...], w_ref[...], preferred_element_type=F32)
    g = jax.nn.gelu(h)
    ubuf[...] = g[:, :width]
    vbuf[...] = _layernorm(g[:, width:], lw_ref[...], lb_ref[...]).astype(BF16)
    row = lax.broadcasted_iota(I32, (GMLP_CHUNK, GMLP_CHUNK), 0)
    col = lax.broadcasted_iota(I32, (GMLP_CHUNK, GMLP_CHUNK), 1)
    for gi in range(GMLP_GROUPS):
        w_c = jnp.where(row >= col, ws_ref[gi], 0.0).astype(BF16)
        cols = slice(gi * gc, (gi + 1) * gc)
        for c in range(tile // GMLP_CHUNK):
            rows = slice(c * GMLP_CHUNK, (c + 1) * GMLP_CHUNK)
            mixed = jnp.dot(w_c, vbuf[rows, cols], preferred_element_type=F32) + bs_ref[gi]
            o_ref[rows, cols] = (ubuf[rows, cols] * mixed).astype(o_ref.dtype)


def _gmlp(xn, w, ln_w, ln_b, w_s, b_s):
    width = BRANCH_WIDTH
    gc = width // GMLP_GROUPS
    b_full = jnp.broadcast_to(b_s[:, :, None], (GMLP_GROUPS, GMLP_CHUNK, gc))
    weights = [w, ln_w.reshape(1, width), ln_b.reshape(1, width), w_s, b_full]
    scratch = lambda tile: [pltpu.VMEM((tile, width), F32), pltpu.VMEM((tile, width), BF16)]
    return _seq_call(_gmlp_body, xn, weights, width, scratch, "gmlp")


def _softplus(x):
    return jnp.maximum(x, 0.0) + jnp.log(1.0 + jnp.exp(-jnp.abs(x)))


def _bmm(a, b):
    return jnp.einsum("nij,njk->nik", a.astype(BF16), b.astype(BF16), preferred_element_type=F32)


def _bmm_nt(a, b):
    return jnp.einsum("nid,njd->nij", a.astype(BF16), b.astype(BF16), preferred_element_type=F32)


def _bmm_tn(a, b):
    return jnp.einsum("ncd,nce->nde", a.astype(BF16), b.astype(BF16), preferred_element_type=F32)


def _unit_lower_inverse(strict_lower, eye):
    size = strict_lower.shape[-1]
    inv = eye - strict_lower
    power = strict_lower
    span = 2
    while span < size:
        power = _bmm(power, power)
        inv = inv + _bmm(inv, power)
        span *= 2
    return inv


def _split3(x):
    hi = x.astype(BF16)
    r1 = x - hi.astype(F32)
    mid = r1.astype(BF16)
    lo = (r1 - mid.astype(F32)).astype(BF16)
    return hi, mid, lo


def _gdn_body(xn_ref, w_ref, wb_ref, wa_ref, cw_ref, alog_ref, dtb_ref, nw_ref, o_ref,
              hbuf, qbuf, kbuf, vbuf, zbuf, gbuf, bbuf, s_ref, wc_ref, bc_ref, qp_ref, op_ref, egl_ref):
    tile = o_ref.shape[0]
    chunk = GDN_CHUNK
    t = pl.program_id(1)
    _carry_halo(hbuf, SMALL_HALO, tile)

    @pl.when(t == 0)
    def _():
        s_ref[...] = jnp.zeros(s_ref.shape, F32)

    xn = xn_ref[...]
    h = jnp.dot(xn, w_ref[...], preferred_element_type=F32)
    n_conv = 2 * GDN_QK + GDN_V
    hbuf[SMALL_HALO:SMALL_HALO + tile, :] = h[:, :n_conv]
    zbuf[...] = h[:, n_conv:]
    bbuf[...] = _sigmoid(jnp.dot(xn, wb_ref[...], preferred_element_type=F32))
    a_raw = jnp.dot(xn, wa_ref[...], preferred_element_type=F32)
    gbuf[...] = -jnp.exp(alog_ref[...]) * _softplus(a_raw + dtb_ref[...])

    for r in range(tile // CONV_ROWS):
        rows = slice(r * CONV_ROWS, (r + 1) * CONV_ROWS)
        init = jnp.zeros((CONV_ROWS, n_conv), F32)
        qkv = _silu(_causal_taps(hbuf, cw_ref, SMALL_HALO, r * CONV_ROWS, CONV_ROWS, init))
        for hd in range(GDN_HEADS):
            q = qkv[:, hd * GDN_DK:(hd + 1) * GDN_DK]
            k = qkv[:, GDN_QK + hd * GDN_DK:GDN_QK + (hd + 1) * GDN_DK]
            q = q * lax.rsqrt(jnp.sum(q * q, -1, keepdims=True) + EPS) * (GDN_DK ** -0.5)
            k = k * lax.rsqrt(jnp.sum(k * k, -1, keepdims=True) + EPS)
            qbuf[rows, hd * GDN_DK:(hd + 1) * GDN_DK] = q
            kbuf[rows, hd * GDN_DK:(hd + 1) * GDN_DK] = k
        vbuf[rows, :] = qkv[:, 2 * GDN_QK:]

    n_chunks = tile // chunk
    n_batch = GDN_HEADS * n_chunks
    row = lax.broadcasted_iota(I32, (chunk, chunk), 0)
    col = lax.broadcasted_iota(I32, (chunk, chunk), 1)
    tril = row >= col
    strict = row > col
    eye = jnp.where(row == col, 1.0, 0.0)

    def per_head(buf, width):
        return jnp.concatenate(
            [buf[:, hd * width:(hd + 1) * width].reshape(n_chunks, chunk, width)
             for hd in range(GDN_HEADS)], axis=0)

    g_lo, g_mid, g_hi = reversed(_split3(gbuf[...].reshape(n_chunks, chunk, LANES)))
    ones_tril = jnp.broadcast_to(jnp.where(tril, 1.0, 0.0).astype(BF16), (n_chunks, chunk, chunk))
    g_cum = _bmm(ones_tril, g_lo) + _bmm(ones_tril, g_mid) + _bmm(ones_tril, g_hi)
    g_cum_t = [g_cum[c].T for c in range(n_chunks)]
    g_col = jnp.concatenate([g_cum[:, :, hd:hd + 1] for hd in range(GDN_HEADS)], axis=0)
    g_row = jnp.concatenate(
        [g_cum_t[c][hd:hd + 1, :][None] for hd in range(GDN_HEADS) for c in range(n_chunks)], axis=0)
    b3 = bbuf[...].reshape(n_chunks, chunk, LANES)
    b_col = jnp.concatenate([b3[:, :, hd:hd + 1] for hd in range(GDN_HEADS)], axis=0)
    g_last = g_col[:, chunk - 1:chunk, :]

    q = per_head(qbuf[...], GDN_DK)
    k = per_head(kbuf[...], GDN_DK)
    v = per_head(vbuf[...], GDN_DV)
    decay = jnp.where(tril, jnp.exp(jnp.where(tril, g_col - g_row, 0.0)), 0.0)
    e_g = jnp.exp(g_col)
    k_beta = k * b_col
    inv = _unit_lower_inverse(jnp.where(strict, _bmm_nt(k_beta, k) * decay, 0.0), eye)
    uw = _bmm(inv, jnp.concatenate([v * b_col, k_beta * e_g], axis=2))
    u = uw[:, :, :GDN_DV]
    w = uw[:, :, GDN_DV:]
    attn = _bmm_nt(q, k) * decay
    k_dec = k * jnp.exp(g_last - g_col)
    wc_ref[...] = _bmm_tn(k_dec, w).astype(BF16)
    bc_ref[...] = _bmm_tn(k_dec, u)
    qp_ref[...] = (q * e_g - _bmm(attn, w)).astype(BF16)
    op_ref[...] = _bmm(attn, u)
    egl_ref[...] = jnp.broadcast_to(jnp.exp(g_last), (n_batch, 1, GDN_DV))

    for c in range(n_chunks):
        rows = slice(c * chunk, (c + 1) * chunk)
        for hd in range(GDN_HEADS):
            n = hd * n_chunks + c
            cols = slice(hd * GDN_DV, (hd + 1) * GDN_DV)
            state = s_ref[hd]
            s16 = state.astype(BF16)
            o = jnp.dot(qp_ref[n], s16, preferred_element_type=F32) + op_ref[n]
            s_ref[hd] = (state * egl_ref[n] + bc_ref[n]
                         - jnp.dot(wc_ref[n], s16, preferred_element_type=F32))
            o = o * lax.rsqrt(jnp.mean(o * o, -1, keepdims=True) + EPS) * nw_ref[...]
            o_ref[rows, cols] = (o * _silu(zbuf[rows, cols])).astype(o_ref.dtype)


def _lane_pad(vec):
    return jnp.zeros((1, LANES), F32).at[0, :vec.shape[0]].set(vec)


def _gdn(xn, w, w_beta, w_a, conv_w, a_log, dt_bias, norm_w):
    pad = ((0, 0), (0, LANES - GDN_HEADS))
    weights = [w, jnp.pad(w_beta, pad), jnp.pad(w_a, pad), conv_w,
               _lane_pad(a_log), _lane_pad(dt_bias), norm_w.reshape(1, GDN_DV)]
    n_conv = 2 * GDN_QK + GDN_V

    def scratch(tile):
        n_batch = GDN_HEADS * (tile // GDN_CHUNK)
        return [pltpu.VMEM((SMALL_HALO + tile, n_conv), F32),
                pltpu.VMEM((tile, GDN_QK), F32), pltpu.VMEM((tile, GDN_QK), F32),
                pltpu.VMEM((tile, GDN_V), F32), pltpu.VMEM((tile, GDN_V), F32),
                pltpu.VMEM((tile, LANES), F32), pltpu.VMEM((tile, LANES), F32),
                pltpu.VMEM((GDN_HEADS, GDN_DK, GDN_DV), F32),
                pltpu.VMEM((n_batch, GDN_DK, GDN_DK), BF16),
                pltpu.VMEM((n_batch, GDN_DK, GDN_DV), F32),
                pltpu.VMEM((n_batch, GDN_CHUNK, GDN_DK), BF16),
                pltpu.VMEM((n_batch, GDN_CHUNK, GDN_DV), F32),
                pltpu.VMEM((n_batch, 1, GDN_DV), F32)]

    return _seq_call(_gdn_body, xn, weights, GDN_V, scratch, "gdn")


def _merge_body(x_ref, xn_ref, y0_ref, y1_ref, y2_ref, y3_ref, wg_ref, wb_ref, wo_ref, o_ref):
    xn = xn_ref[...]
    merged = None
    for i, y_ref in enumerate((y0_ref, y1_ref, y2_ref, y3_ref)):
        gate = _sigmoid(jnp.dot(xn, wg_ref[i], preferred_element_type=F32))
        term = gate * jnp.dot(y_ref[...], wb_ref[i], preferred_element_type=F32)
        merged = term if merged is None else merged + term
    o_ref[...] = x_ref[...] + jnp.dot(merged.astype(BF16), wo_ref[...], preferred_element_type=F32)


def _merge(x2d, xn2d, ys, w_gate, w_branch, w_out):
    n, d = x2d.shape
    tm = min(ROW_TILE, n)
    row = lambda width: pl.BlockSpec((tm, width), lambda i: (i, 0))
    return pl.pallas_call(
        _merge_body,
        grid=(n // tm,),
        in_specs=[row(d), row(d)] + [row(BRANCH_WIDTH)] * N_BRANCH
        + [_full(w_gate.shape), _full(w_branch.shape), _full(w_out.shape)],
        out_specs=row(d),
        out_shape=jax.ShapeDtypeStruct((n, d), F32),
        compiler_params=_params(("parallel",)),
        name="merge",
    )(x2d, xn2d, *ys, w_gate, w_branch, w_out)


def _split2(x):
    hi = x.astype(BF16)
    lo = (x - hi.astype(F32)).astype(BF16)
    return hi, lo


def _router_body(x_ref, nw_ref, wr_ref, xn_ref, info_ref, cnt_ref):
    tm = x_ref.shape[0]

    @pl.when(pl.program_id(0) == 0)
    def _():
        cnt_ref[...] = jnp.zeros(cnt_ref.shape, F32)

    x = x_ref[...]
    xn = x * lax.rsqrt(jnp.mean(x * x, -1, keepdims=True) + EPS) * nw_ref[...]
    xn_ref[...] = xn
    x_hi, x_lo = _split2(xn)
    logits = (jnp.dot(x_lo, wr_ref[0], preferred_element_type=F32)
              + jnp.dot(x_hi, wr_ref[1], preferred_element_type=F32)
              + jnp.dot(x_hi, wr_ref[0], preferred_element_type=F32))

    lane = lax.broadcasted_iota(I32, (tm, LANES), 1)
    neg = jnp.float32(-jnp.inf)
    big = jnp.int32(2 * LANES)
    is_grp = (lane >= GRP_LANE0) & (lane < GRP_LANE0 + MOE_GROUPS)
    lg = jnp.where(is_grp, logits, neg)
    mg = jnp.max(lg, -1, keepdims=True)
    grp = jnp.min(jnp.where(lg == mg, lane, big), -1, keepdims=True) - GRP_LANE0
    p_sel = 1.0 / jnp.sum(jnp.where(is_grp, jnp.exp(lg - mg), 0.0), -1, keepdims=True)

    in_grp = (lane < N_EXPERTS) & ((lane // MOE_PER_GROUP) == grp)
    le = jnp.where(in_grp, logits, neg)
    m0 = jnp.max(le, -1, keepdims=True)
    e0 = jnp.min(jnp.where(le == m0, lane, big), -1, keepdims=True)
    le1 = jnp.where(lane == e0, neg, le)
    m1 = jnp.max(le1, -1, keepdims=True)
    e1 = jnp.min(jnp.where(le1 == m1, lane, big), -1, keepdims=True)
    r = jnp.exp(m1 - m0)
    w0 = p_sel / (1.0 + r)
    w1 = p_sel * r / (1.0 + r)

    hot0 = lane == e0
    hot1 = lane == e1
    onehot = jnp.where(hot0 | hot1, 1.0, 0.0)
    ti = lax.broadcasted_iota(I32, (tm, tm), 0)
    tj = lax.broadcasted_iota(I32, (tm, tm), 1)
    before = jnp.where(ti > tj, 1.0, 0.0).astype(BF16)
    base = jnp.dot(before, onehot.astype(BF16), preferred_element_type=F32) + cnt_ref[...]
    r0 = jnp.sum(jnp.where(hot0, base, 0.0), -1, keepdims=True)
    r1 = jnp.sum(jnp.where(hot1, base, 0.0), -1, keepdims=True)
    cnt_ref[...] = cnt_ref[...] + jnp.sum(onehot, 0, keepdims=True)

    info = jnp.zeros((tm, LANES), F32)
    for idx, val in ((INFO_E0, e0.astype(F32)), (INFO_E1, e1.astype(F32)), (INFO_W0, w0),
                     (INFO_W1, w1), (INFO_R0, r0), (INFO_R1, r1)):
        info = jnp.where(lane == idx, val, info)
    info_ref[...] = info


def _router(x2d, norm_w, router_grp, router_exp):
    n, d = x2d.shape
    tm = min(ROW_TILE, n)
    w_r = jnp.zeros((d, LANES), F32).at[:, :N_EXPERTS].set(router_exp)
    w_r = w_r.at[:, GRP_LANE0:GRP_LANE0 + MOE_GROUPS].set(router_grp)
    w_parts = jnp.stack(_split2(w_r))
    return pl.pallas_call(
        _router_body,
        grid=(n // tm,),
        in_specs=[pl.BlockSpec((tm, d), lambda i: (i, 0)), _full((1, d)), _full(w_parts.shape)],
        out_specs=[pl.BlockSpec((tm, d), lambda i: (i, 0)),
                   pl.BlockSpec((tm, LANES), lambda i: (i, 0)),
                   _full((1, LANES))],
        out_shape=[jax.ShapeDtypeStruct((n, d), F32),
                   jax.ShapeDtypeStruct((n, LANES), F32),
                   jax.ShapeDtypeStruct((1, LANES), F32)],
        compiler_params=_params(("arbitrary",)),
        name="router",
    )(x2d, norm_w.reshape(1, d), w_parts)


RANK_BITS = 16
SLOT_FREE = -1
INVERT_UNROLL = 8


INVERT_STEP = 1024


def _invert_body(code_ref, start_ref, free_hbm, slot_ref, sem):
    step = pl.program_id(0)

    @pl.when(step == 0)
    def _():
        fill = pltpu.make_async_copy(free_hbm, slot_ref, sem)
        fill.start()
        fill.wait()

    def body(i, carry):
        code = code_ref[i]
        expert = lax.shift_right_logical(code, RANK_BITS)
        rank = code & ((1 << RANK_BITS) - 1)
        slot_ref[start_ref[expert] + rank] = step * INVERT_STEP + i
        return carry

    lax.fori_loop(0, INVERT_STEP, body, 0, unroll=INVERT_UNROLL)


def _invert(code, pad_start, n_rows):
    smem = pl.BlockSpec(memory_space=pltpu.SMEM)
    return pl.pallas_call(
        _invert_body,
        grid=(code.shape[0] // INVERT_STEP,),
        in_specs=[pl.BlockSpec((INVERT_STEP,), lambda i: (i,), memory_space=pltpu.SMEM), smem,
                  pl.BlockSpec(memory_space=pl.ANY)],
        out_specs=smem,
        out_shape=jax.ShapeDtypeStruct((n_rows,), I32),
        scratch_shapes=[pltpu.SemaphoreType.DMA(())],
        compiler_params=_params(("arbitrary",)),
        name="invert",
    )(code, pad_start, jnp.full((n_rows,), SLOT_FREE, I32))


def _row_copies(idx_ref, hbm, buf, sem, to_vmem):
    copies = []
    for r in range(MOE_BLOCK):
        hbm_row = hbm.at[pl.ds(idx_ref[0, r], 1)]
        buf_row = buf.at[pl.ds(r, 1)]
        src, dst = (hbm_row, buf_row) if to_vmem else (buf_row, hbm_row)
        copies.append(pltpu.make_async_copy(src, dst, sem))
    return copies


DMA_THREADS = 2


def _start_all(copies):
    for r, cp in enumerate(copies):
        cp.start(priority=r % DMA_THREADS)


def _ffn_body(blk_ref, last_ref, gcur_ref, gnext_ref, sprev_ref, scur_ref, x_hbm,
              wg_ref, wu_ref, wd_ref, yt_hbm, xbuf, ybuf, wg16, wu16, wd16, gsem, ssem):
    b = pl.program_id(0)
    last = last_ref[0]

    @pl.when(b <= last)
    def _():
        slot = b % 2
        other = 1 - slot

        @pl.when(b == 0)
        def _():
            ybuf[...] = jnp.zeros(ybuf.shape, F32)
            _start_all(_row_copies(gcur_ref, x_hbm, xbuf.at[0], gsem.at[0], True))

        @pl.when(b > 0)
        def _():
            for cp in _row_copies(sprev_ref, yt_hbm, ybuf.at[slot], ssem.at[slot], False):
                cp.wait()

        for cp in _row_copies(gcur_ref, x_hbm, xbuf.at[slot], gsem.at[slot], True):
            cp.wait()

        @pl.when((b == 0) | (blk_ref[b] != blk_ref[jnp.maximum(b - 1, 0)]))
        def _():
            wg16[...] = wg_ref[...].astype(BF16)
            wu16[...] = wu_ref[...].astype(BF16)
            wd16[...] = wd_ref[...].astype(BF16)

        _start_all(_row_copies(gnext_ref, x_hbm, xbuf.at[other], gsem.at[other], True))
        _start_all(_row_copies(scur_ref, yt_hbm, ybuf.at[other], ssem.at[other], False))
        xb = xbuf[slot].astype(BF16)
        g = jnp.dot(xb, wg16[...], preferred_element_type=F32)
        u = jnp.dot(xb, wu16[...], preferred_element_type=F32)
        hb = (_silu(g) * u).astype(BF16)
        ybuf[slot] = jnp.dot(hb, wd16[...], preferred_element_type=F32)

        @pl.when(b == last)
        def _():
            for cp in _row_copies(gnext_ref, x_hbm, xbuf.at[other], gsem.at[other], True):
                cp.wait()
            for cp in _row_copies(scur_ref, yt_hbm, ybuf.at[other], ssem.at[other], False):
                cp.wait()


def _ffn(xn2d, gather_rows, scatter_rows, blk_expert, last_step, n_out_rows, layer,
         w_gate, w_up, w_down):
    _, d = xn2d.shape
    de = w_gate.shape[-1]
    n_steps = blk_expert.shape[0]
    idx_spec = lambda shift: pl.BlockSpec(
        (None, 1, MOE_BLOCK), lambda b, e, u: (jnp.maximum(b + shift, 0), 0, 0),
        memory_space=pltpu.SMEM)
    w_spec = lambda rows, cols: pl.BlockSpec((None, None, rows, cols),
                                             lambda b, e, u: (layer, e[b], 0, 0))
    grid_spec = pltpu.PrefetchScalarGridSpec(
        num_scalar_prefetch=2,
        grid=(n_steps,),
        in_specs=[idx_spec(0), idx_spec(1), idx_spec(-1), idx_spec(0),
                  pl.BlockSpec(memory_space=pl.ANY),
                  w_spec(d, de), w_spec(d, de), w_spec(de, d)],
        out_specs=pl.BlockSpec(memory_space=pl.ANY),
        scratch_shapes=[pltpu.VMEM((2, MOE_BLOCK, d), F32), pltpu.VMEM((2, MOE_BLOCK, d), F32),
                        pltpu.VMEM((d, de), BF16), pltpu.VMEM((d, de), BF16),
                        pltpu.VMEM((de, d), BF16),
                        pltpu.SemaphoreType.DMA((2,)), pltpu.SemaphoreType.DMA((2,))],
    )
    return pl.pallas_call(
        _ffn_body,
        grid_spec=grid_spec,
        out_shape=jax.ShapeDtypeStruct((n_out_rows, d), F32),
        compiler_params=_params(("arbitrary",)),
        name="expert_ffn",
    )(blk_expert, last_step, gather_rows, gather_rows, scatter_rows, scatter_rows, xn2d,
      w_gate, w_up, w_down)


def _combine_body(x_ref, y0_ref, y1_ref, info_ref, nw_ref, *out_refs):
    info = info_ref[...]
    x = (x_ref[...] + info[:, INFO_W0:INFO_W0 + 1] * y0_ref[...]
         + info[:, INFO_W1:INFO_W1 + 1] * y1_ref[...])
    xn = x * lax.rsqrt(jnp.mean(x * x, -1, keepdims=True) + EPS) * nw_ref[...]
    if len(out_refs) == 2:
        out_refs[0][...] = x
    out_refs[-1][...] = xn.astype(out_refs[-1].dtype)


def _combine(x2d, yt, info, next_norm_w, final):
    n, d = x2d.shape
    tm = MOE_BLOCK
    row = lambda width: pl.BlockSpec((tm, width), lambda i: (i, 0))
    slot1 = pl.BlockSpec((tm, d), lambda i: (i + n // tm + 1, 0))
    out_shape = [jax.ShapeDtypeStruct((n, d), F32)]
    if not final:
        out_shape.append(jax.ShapeDtypeStruct((n, d), BF16))
    return pl.pallas_call(
        _combine_body,
        grid=(n // tm,),
        in_specs=[row(d), row(d), slot1, row(LANES), _full((1, d))],
        out_specs=[row(d)] * len(out_shape),
        out_shape=out_shape,
        compiler_params=_params(("parallel",)),
        name="combine",
    )(x2d, yt, yt, info, next_norm_w.reshape(1, d))


def _hier_moe(x2d, norm_w, router_grp, router_exp, layer, w_gate, w_up, w_down, next_norm_w, final):
    n, _ = x2d.shape
    xn, info, cnt = _router(x2d, norm_w, router_grp, router_exp)
    counts = cnt[0, :N_EXPERTS].astype(I32)
    padded = (counts + MOE_BLOCK - 1) // MOE_BLOCK * MOE_BLOCK
    pad_end = jnp.cumsum(padded)
    pad_start = pad_end - padded
    expert = info[:, INFO_E0:INFO_E1 + 1].astype(I32)
    rank = info[:, INFO_R0:INFO_R1 + 1].astype(I32)
    code = ((expert << RANK_BITS) | rank).reshape(-1)
    n_blocks = (2 * n + N_EXPERTS * (MOE_BLOCK - 1)) // MOE_BLOCK + 1
    slots = _invert(code, pad_start, n_blocks * MOE_BLOCK)
    free = slots < 0
    spare = n + jnp.arange(MOE_BLOCK, dtype=I32)
    gather_rows = jnp.where(free, 0, slots >> 1)
    scatter_rows = jnp.where(free, jnp.tile(spare, n_blocks),
                             (slots & 1) * (n + MOE_BLOCK) + (slots >> 1))
    gather_rows = jnp.concatenate([gather_rows, jnp.zeros((2 * MOE_BLOCK,), I32)])
    scatter_rows = jnp.concatenate([spare, scatter_rows, spare])
    blk_start = jnp.arange(n_blocks + 1, dtype=I32) * MOE_BLOCK
    blk_expert = jnp.minimum(jnp.sum((pad_end[None, :] <= blk_start[:, None]).astype(I32), axis=1),
                             N_EXPERTS - 1)
    last_step = (pad_end[-1:] // MOE_BLOCK).astype(I32)
    yt = _ffn(xn, gather_rows.reshape(n_blocks + 2, 1, MOE_BLOCK),
              scatter_rows.reshape(n_blocks + 2, 1, MOE_BLOCK), blk_expert, last_step,
              2 * n + MOE_BLOCK, layer, w_gate, w_up, w_down)
    return _combine(x2d, yt, info, next_norm_w, final)


def _mixer_block(x2d, xn2d, bsz, w_in, gdn_conv_w, gdn_a_log, gdn_dt_bias, gdn_norm_w,
                 conf_conv_w, conf_conv_b, conf_ln_w, conf_ln_b, gmlp_ln_w, gmlp_ln_b,
                 gmlp_w_s, gmlp_b_s, sc_conv_w, w_branch, w_out):
    n, d = x2d.shape
    t_len = n // bsz
    xn = xn2d.reshape(bsz, t_len, d)
    c0 = 2 * GDN_QK + 2 * GDN_V
    c1 = c0 + GDN_HEADS
    c2 = c1 + GDN_HEADS
    c3 = c2 + 2 * BRANCH_WIDTH
    c4 = c3 + 2 * BRANCH_WIDTH
    c5 = c4 + 3 * BRANCH_WIDTH
    w16 = w_in.astype(BF16)
    ys = (
        _gdn(xn, w16[:, :c0], w16[:, c0:c1], w16[:, c1:c2], gdn_conv_w, gdn_a_log, gdn_dt_bias,
             gdn_norm_w),
        _conformer(xn, w16[:, c2:c3], conf_conv_w, conf_conv_b, conf_ln_w, conf_ln_b),
        _gmlp(xn, w16[:, c3:c4], gmlp_ln_w, gmlp_ln_b, gmlp_w_s, gmlp_b_s),
        _shortconv(xn, w16[:, c4:c5], sc_conv_w),
    )
    ys = [y.reshape(bsz * t_len, BRANCH_WIDTH) for y in ys]
    w_gate = w16[:, c5:].reshape(d, N_BRANCH, d).transpose(1, 0, 2)
    return _merge(x2d, xn2d, ys, w_gate, w_branch.astype(BF16), w_out.astype(BF16))


def kernel(x, mix_norm_w, w_in, gdn_conv_w, gdn_A_log, gdn_dt_bias, gdn_norm_w, conf_conv_w,
           conf_conv_b, conf_ln_w, conf_ln_b, gmlp_ln_w, gmlp_ln_b, gmlp_w_s, gmlp_b_s, sc_conv_w,
           w_branch, w_out, moe_norm_w, router_grp, router_exp, w_gate, w_up, w_down, final_norm_w):
    bsz, t_len, d = x.shape
    depth = mix_norm_w.shape[0]
    x2d = x.reshape(bsz * t_len, d)
    xn2d = _rmsnorm(x2d, mix_norm_w[0], BF16)
    for l in range(depth):
        x2d = _mixer_block(x2d, xn2d, bsz, w_in[l], gdn_conv_w[l], gdn_A_log[l], gdn_dt_bias[l],
                           gdn_norm_w[l], conf_conv_w[l], conf_conv_b[l], conf_ln_w[l],
                           conf_ln_b[l], gmlp_ln_w[l], gmlp_ln_b[l], gmlp_w_s[l], gmlp_b_s[l],
                           sc_conv_w[l], w_branch[l], w_out[l])
        final = l == depth - 1
        next_norm_w = final_norm_w if final else mix_norm_w[l + 1]
        outs = _hier_moe(x2d, moe_norm_w[l], router_grp[l], router_exp[l], l, w_gate, w_up, w_down,
                         next_norm_w, final)
        if final:
            return outs[0].reshape(bsz, t_len, d)
        x2d, xn2d = outs
```

```python
import functools

import jax
import jax.numpy as jnp
from jax import lax
from jax.experimental import pallas as pl
from jax.experimental.pallas import tpu as pltpu

F32 = jnp.float32
BF16 = jnp.bfloat16
I32 = jnp.int32

D_MODEL = 1024
BRANCH_WIDTH = 512
N_BRANCH = 4
GDN_HEADS = 4
GDN_DK = 128
GDN_DV = 128
GDN_CONV = 4
GDN_CHUNK = 64
GDN_QK = GDN_HEADS * GDN_DK
GDN_V = GDN_HEADS * GDN_DV
CONF_KERNEL = 31
GMLP_GROUPS = 4
GMLP_CHUNK = 128
SC_KERNEL = 3
MOE_GROUPS = 8
MOE_PER_GROUP = 8
N_EXPERTS = MOE_GROUPS * MOE_PER_GROUP
D_EXPERT = 512
MOE_BLOCK = 256
EPS = 1e-6

LANES = 128
SUBLANES = 8
VMEM_LIMIT = 56 * 1024 * 1024

SEQ_TILE = 512
ROW_TILE = 512
CONF_HALO = 32
SMALL_HALO = SUBLANES
CONV_ROWS = 64

INFO_E0, INFO_E1, INFO_W0, INFO_W1, INFO_R0, INFO_R1 = range(6)
GRP_LANE0 = N_EXPERTS


def _params(sem):
    return pltpu.CompilerParams(dimension_semantics=sem, vmem_limit_bytes=VMEM_LIMIT)


def _full(shape):
    nd = len(shape)
    return pl.BlockSpec(shape, lambda *_: (0,) * nd)


def _bdot(a, b):
    return jnp.dot(a.astype(BF16), b.astype(BF16), preferred_element_type=F32)


def _sigmoid(x):
    return 1.0 / (1.0 + jnp.exp(-x))


def _silu(x):
    return x * _sigmoid(x)


def _rms_body(x_ref, w_ref, o_ref):
    x = x_ref[...]
    ms = jnp.mean(x * x, -1, keepdims=True)
    o_ref[...] = (x * lax.rsqrt(ms + EPS) * w_ref[...]).astype(o_ref.dtype)


def _rmsnorm(x2d, w, out_dtype):
    n, d = x2d.shape
    tm = min(ROW_TILE, n)
    return pl.pallas_call(
        _rms_body,
        grid=(n // tm,),
        in_specs=[pl.BlockSpec((tm, d), lambda i: (i, 0)), _full((1, d))],
        out_specs=pl.BlockSpec((tm, d), lambda i: (i, 0)),
        out_shape=jax.ShapeDtypeStruct((n, d), out_dtype),
        compiler_params=_params(("parallel",)),
        name="rmsnorm",
    )(x2d, w.reshape(1, d))


def _carry_halo(buf, halo, tile):
    t = pl.program_id(1)

    @pl.when(t == 0)
    def _():
        buf[0:halo, :] = jnp.zeros((halo, buf.shape[1]), buf.dtype)

    @pl.when(t != 0)
    def _():
        buf[0:halo, :] = buf[tile:tile + halo, :]


def _causal_taps(buf, cw_ref, halo, row0, rows, init):
    k_w = cw_ref.shape[0]
    acc = init
    for k in range(k_w):
        off = halo - (k_w - 1) + k + row0
        acc = acc + cw_ref[k:k + 1, :] * buf[off:off + rows, :]
    return acc


def _causal_taps_by_phase(buf, cw_ref, halo, row0, rows, init):
    k_w = cw_ref.shape[0]
    base = halo - (k_w - 1)
    acc = init
    for phase in range(SUBLANES):
        taps = [k for k in range(k_w) if (base + k) % SUBLANES == phase]
        span = rows + (SUBLANES if phase else 0)
        part = None
        for k in taps:
            off = base + k - phase + row0
            term = cw_ref[k:k + 1, :] * buf[off:off + span, :]
            part = term if part is None else part + term
        if part is not None:
            acc = acc + part[phase:phase + rows, :]
    return acc


def _layernorm(x, w, b):
    mu = jnp.mean(x, -1, keepdims=True)
    xc = x - mu
    var = jnp.mean(xc * xc, -1, keepdims=True)
    return xc * lax.rsqrt(var + EPS) * w + b


def _seq_call(body, xn, weights, out_width, scratch, name):
    bsz, t_len, d = xn.shape
    tile = min(SEQ_TILE, t_len)
    in_specs = [pl.BlockSpec((None, tile, d), lambda b, t: (b, t, 0))]
    in_specs += [_full(w.shape) for w in weights]
    return pl.pallas_call(
        body,
        grid=(bsz, t_len // tile),
        in_specs=in_specs,
        out_specs=pl.BlockSpec((None, tile, out_width), lambda b, t: (b, t, 0)),
        out_shape=jax.ShapeDtypeStruct((bsz, t_len, out_width), BF16),
        scratch_shapes=scratch(tile),
        compiler_params=_params(("parallel", "arbitrary")),
        name=name,
    )(xn, *weights)


def _conf_body(xn_ref, w_ref, cw_ref, cb_ref, lw_ref, lb_ref, o_ref, ybuf):
    tile, width = o_ref.shape
    _carry_halo(ybuf, CONF_HALO, tile)
    h = jnp.dot(xn_ref[...], w_ref[...], preferred_element_type=F32)
    ybuf[CONF_HALO:CONF_HALO + tile, :] = h[:, :width] * _sigmoid(h[:, width:])
    for r in range(tile // CONV_ROWS):
        init = jnp.broadcast_to(cb_ref[...], (CONV_ROWS, width))
        y = _causal_taps_by_phase(ybuf, cw_ref, CONF_HALO, r * CONV_ROWS, CONV_ROWS, init)
        y = _layernorm(y, lw_ref[...], lb_ref[...])
        o_ref[r * CONV_ROWS:(r + 1) * CONV_ROWS, :] = _silu(y).astype(o_ref.dtype)


def _conformer(xn, w, conv_w, conv_b, ln_w, ln_b):
    width = BRANCH_WIDTH
    weights = [w, conv_w, conv_b.reshape(1, width), ln_w.reshape(1, width), ln_b.reshape(1, width)]
    scratch = lambda tile: [pltpu.VMEM((CONF_HALO + tile, width), F32)]
    return _seq_call(_conf_body, xn, weights, width, scratch, "conformer")


def _sc_body(xn_ref, w_ref, cw_ref, o_ref, ubuf, bbuf):
    tile, width = o_ref.shape
    _carry_halo(ubuf, SMALL_HALO, tile)
    h = jnp.dot(xn_ref[...], w_ref[...], preferred_element_type=F32)
    bbuf[...] = h[:, :width]
    ubuf[SMALL_HALO:SMALL_HALO + tile, :] = h[:, width:2 * width] * h[:, 2 * width:]
    for r in range(tile // CONV_ROWS):
        rows = slice(r * CONV_ROWS, (r + 1) * CONV_ROWS)
        init = jnp.zeros((CONV_ROWS, width), F32)
        y = _causal_taps(ubuf, cw_ref, SMALL_HALO, r * CONV_ROWS, CONV_ROWS, init)
        o_ref[rows, :] = (bbuf[rows, :] * y).astype(o_ref.dtype)


def _shortconv(xn, w, conv_w):
    width = BRANCH_WIDTH
    scratch = lambda tile: [pltpu.VMEM((SMALL_HALO + tile, width), F32), pltpu.VMEM((tile, width), F32)]
    return _seq_call(_sc_body, xn, [w, conv_w], width, scratch, "shortconv")


def _gmlp_body(xn_ref, w_ref, lw_ref, lb_ref, ws_ref, bs_ref, o_ref, ubuf, vbuf):
    tile, width = o_ref.shape
    gc = width // GMLP_GROUPS
    h = jnp.dot(xn_ref[...], w_ref[...], preferred_element_type=F32)
    g = jax.nn.gelu(h)
    ubuf[...] = g[:, :width]
    vbuf[...] = _layernorm(g[:, width:], lw_ref[...], lb_ref[...]).astype(BF16)
    row = lax.broadcasted_iota(I32, (GMLP_CHUNK, GMLP_CHUNK), 0)
    col = lax.broadcasted_iota(I32, (GMLP_CHUNK, GMLP_CHUNK), 1)
    for gi in range(GMLP_GROUPS):
        w_c = jnp.where(row >= col, ws_ref[gi], 0.0).astype(BF16)
        cols = slice(gi * gc, (gi + 1) * gc)
        for c in range(tile // GMLP_CHUNK):
            rows = slice(c * GMLP_CHUNK, (c + 1) * GMLP_CHUNK)
            mixed = jnp.dot(w_c, vbuf[rows, cols], preferred_element_type=F32) + bs_ref[gi]
            o_ref[rows, cols] = (ubuf[rows, cols] * mixed).astype(o_ref.dtype)


def _gmlp(xn, w, ln_w, ln_b, w_s, b_s):
    width = BRANCH_WIDTH
    gc = width // GMLP_GROUPS
    b_full = jnp.broadcast_to(b_s[:, :, None], (GMLP_GROUPS, GMLP_CHUNK, gc))
    weights = [w, ln_w.reshape(1, width), ln_b.reshape(1, width), w_s, b_full]
    scratch = lambda tile: [pltpu.VMEM((tile, width), F32), pltpu.VMEM((tile, width), BF16)]
    return _seq_call(_gmlp_body, xn, weights, width, scratch, "gmlp")


def _softplus(x):
    return jnp.maximum(x, 0.0) + jnp.log(1.0 + jnp.exp(-jnp.abs(x)))


def _bmm(a, b):
    return jnp.einsum("nij,njk->nik", a.astype(BF16), b.astype(BF16), preferred_element_type=F32)


def _bmm_nt(a, b):
    return jnp.einsum("nid,njd->nij", a.astype(BF16), b.astype(BF16), preferred_element_type=F32)


def _bmm_tn(a, b):
    return jnp.einsum("ncd,nce->nde", a.astype(BF16), b.astype(BF16), preferred_element_type=F32)


def _unit_lower_inverse(strict_lower, eye):
    size = strict_lower.shape[-1]
    inv = eye - strict_lower
    power = strict_lower
    span = 2
    while span < size:
        power = _bmm(power, power)
        inv = inv + _bmm(inv, power)
        span *= 2
    return inv


def _split3(x):
    hi = x.astype(BF16)
    r1 = x - hi.astype(F32)
    mid = r1.astype(BF16)
    lo = (r1 - mid.astype(F32)).astype(BF16)
    return hi, mid, lo


def _gdn_body(xn_ref, w_ref, wb_ref, wa_ref, cw_ref, alog_ref, dtb_ref, nw_ref, o_ref,
              hbuf, qbuf, kbuf, vbuf, zbuf, gbuf, bbuf, s_ref, wc_ref, bc_ref, qp_ref, op_ref, egl_ref):
    tile = o_ref.shape[0]
    chunk = GDN_CHUNK
    t = pl.program_id(1)
    _carry_halo(hbuf, SMALL_HALO, tile)

    @pl.when(t == 0)
    def _():
        s_ref[...] = jnp.zeros(s_ref.shape, F32)

    xn = xn_ref[...]
    h = jnp.dot(xn, w_ref[...], preferred_element_type=F32)
    n_conv = 2 * GDN_QK + GDN_V
    hbuf[SMALL_HALO:SMALL_HALO + tile, :] = h[:, :n_conv]
    zbuf[...] = h[:, n_conv:]
    bbuf[...] = _sigmoid(jnp.dot(xn, wb_ref[...], preferred_element_type=F32))
    a_raw = jnp.dot(xn, wa_ref[...], preferred_element_type=F32)
    gbuf[...] = -jnp.exp(alog_ref[...]) * _softplus(a_raw + dtb_ref[...])

    for r in range(tile // CONV_ROWS):
        rows = slice(r * CONV_ROWS, (r + 1) * CONV_ROWS)
        init = jnp.zeros((CONV_ROWS, n_conv), F32)
        qkv = _silu(_causal_taps(hbuf, cw_ref, SMALL_HALO, r * CONV_ROWS, CONV_ROWS, init))
        for hd in range(GDN_HEADS):
            q = qkv[:, hd * GDN_DK:(hd + 1) * GDN_DK]
            k = qkv[:, GDN_QK + hd * GDN_DK:GDN_QK + (hd + 1) * GDN_DK]
            q = q * lax.rsqrt(jnp.sum(q * q, -1, keepdims=True) + EPS) * (GDN_DK ** -0.5)
            k = k * lax.rsqrt(jnp.sum(k * k, -1, keepdims=True) + EPS)
            qbuf[rows, hd * GDN_DK:(hd + 1) * GDN_DK] = q
            kbuf[rows, hd * GDN_DK:(hd + 1) * GDN_DK] = k
        vbuf[rows, :] = qkv[:, 2 * GDN_QK:]

    n_chunks = tile // chunk
    n_batch = GDN_HEADS * n_chunks
    row = lax.broadcasted_iota(I32, (chunk, chunk), 0)
    col = lax.broadcasted_iota(I32, (chunk, chunk), 1)
    tril = row >= col
    strict = row > col
    eye = jnp.where(row == col, 1.0, 0.0)

    def per_head(buf, width):
        return jnp.concatenate(
            [buf[:, hd * width:(hd + 1) * width].reshape(n_chunks, chunk, width)
             for hd in range(GDN_HEADS)], axis=0)

    g_lo, g_mid, g_hi = reversed(_split3(gbuf[...].reshape(n_chunks, chunk, LANES)))
    ones_tril = jnp.broadcast_to(jnp.where(tril, 1.0, 0.0).astype(BF16), (n_chunks, chunk, chunk))
    g_cum = _bmm(ones_tril, g_lo) + _bmm(ones_tril, g_mid) + _bmm(ones_tril, g_hi)
    g_cum_t = [g_cum[c].T for c in range(n_chunks)]
    g_col = jnp.concatenate([g_cum[:, :, hd:hd + 1] for hd in range(GDN_HEADS)], axis=0)
    g_row = jnp.concatenate(
        [g_cum_t[c][hd:hd + 1, :][None] for hd in range(GDN_HEADS) for c in range(n_chunks)], axis=0)
    b3 = bbuf[...].reshape(n_chunks, chunk, LANES)
    b_col = jnp.concatenate([b3[:, :, hd:hd + 1] for hd in range(GDN_HEADS)], axis=0)
    g_last = g_col[:, chunk - 1:chunk, :]

    q = per_head(qbuf[...], GDN_DK)
    k = per_head(kbuf[...], GDN_DK)
    v = per_head(vbuf[...], GDN_DV)
    decay = jnp.where(tril, jnp.exp(jnp.where(tril, g_col - g_row, 0.0)), 0.0)
    e_g = jnp.exp(g_col)
    k_beta = k * b_col
    inv = _unit_lower_inverse(jnp.where(strict, _bmm_nt(k_beta, k) * decay, 0.0), eye)
    uw = _bmm(inv, jnp.concatenate([v * b_col, k_beta * e_g], axis=2))
    u = uw[:, :, :GDN_DV]
    w = uw[:, :, GDN_DV:]
    attn = _bmm_nt(q, k) * decay
    k_dec = k * jnp.exp(g_last - g_col)
    wc_ref[...] = _bmm_tn(k_dec, w).astype(BF16)
    bc_ref[...] = _bmm_tn(k_dec, u)
    qp_ref[...] = (q * e_g - _bmm(attn, w)).astype(BF16)
    op_ref[...] = _bmm(attn, u)
    egl_ref[...] = jnp.broadcast_to(jnp.exp(g_last), (n_batch, 1, GDN_DV))

    for c in range(n_chunks):
        rows = slice(c * chunk, (c + 1) * chunk)
        for hd in range(GDN_HEADS):
            n = hd * n_chunks + c
            cols = slice(hd * GDN_DV, (hd + 1) * GDN_DV)
            state = s_ref[hd]
            s16 = state.astype(BF16)
            o = jnp.dot(qp_ref[n], s16, preferred_element_type=F32) + op_ref[n]
            s_ref[hd] = (state * egl_ref[n] + bc_ref[n]
                         - jnp.dot(wc_ref[n], s16, preferred_element_type=F32))
            o = o * lax.rsqrt(jnp.mean(o * o, -1, keepdims=True) + EPS) * nw_ref[...]
            o_ref[rows, cols] = (o * _silu(zbuf[rows, cols])).astype(o_ref.dtype)


def _lane_pad(vec):
    return jnp.zeros((1, LANES), F32).at[0, :vec.shape[0]].set(vec)


def _gdn(xn, w, w_beta, w_a, conv_w, a_log, dt_bias, norm_w):
    pad = ((0, 0), (0, LANES - GDN_HEADS))
    weights = [w, jnp.pad(w_beta, pad), jnp.pad(w_a, pad), conv_w,
               _lane_pad(a_log), _lane_pad(dt_bias), norm_w.reshape(1, GDN_DV)]
    n_conv = 2 * GDN_QK + GDN_V

    def scratch(tile):
        n_batch = GDN_HEADS * (tile // GDN_CHUNK)
        return [pltpu.VMEM((SMALL_HALO + tile, n_conv), F32),
                pltpu.VMEM((tile, GDN_QK), F32), pltpu.VMEM((tile, GDN_QK), F32),
                pltpu.VMEM((tile, GDN_V), F32), pltpu.VMEM((tile, GDN_V), F32),
                pltpu.VMEM((tile, LANES), F32), pltpu.VMEM((tile, LANES), F32),
                pltpu.VMEM((GDN_HEADS, GDN_DK, GDN_DV), F32),
                pltpu.VMEM((n_batch, GDN_DK, GDN_DK), BF16),
                pltpu.VMEM((n_batch, GDN_DK, GDN_DV), F32),
                pltpu.VMEM((n_batch, GDN_CHUNK, GDN_DK), BF16),
                pltpu.VMEM((n_batch, GDN_CHUNK, GDN_DV), F32),
                pltpu.VMEM((n_batch, 1, GDN_DV), F32)]

    return _seq_call(_gdn_body, xn, weights, GDN_V, scratch, "gdn")


def _merge_body(x_ref, xn_ref, y0_ref, y1_ref, y2_ref, y3_ref, wg_ref, wb_ref, wo_ref, o_ref):
    xn = xn_ref[...]
    merged = None
    for i, y_ref in enumerate((y0_ref, y1_ref, y2_ref, y3_ref)):
        gate = _sigmoid(jnp.dot(xn, wg_ref[i], preferred_element_type=F32))
        term = gate * jnp.dot(y_ref[...], wb_ref[i], preferred_element_type=F32)
        merged = term if merged is None else merged + term
    o_ref[...] = x_ref[...] + jnp.dot(merged.astype(BF16), wo_ref[...], preferred_element_type=F32)


def _merge(x2d, xn2d, ys, w_gate, w_branch, w_out):
    n, d = x2d.shape
    tm = min(ROW_TILE, n)
    row = lambda width: pl.BlockSpec((tm, width), lambda i: (i, 0))
    return pl.pallas_call(
        _merge_body,
        grid=(n // tm,),
        in_specs=[row(d), row(d)] + [row(BRANCH_WIDTH)] * N_BRANCH
        + [_full(w_gate.shape), _full(w_branch.shape), _full(w_out.shape)],
        out_specs=row(d),
        out_shape=jax.ShapeDtypeStruct((n, d), F32),
        compiler_params=_params(("parallel",)),
        name="merge",
    )(x2d, xn2d, *ys, w_gate, w_branch, w_out)


def _split2(x):
    hi = x.astype(BF16)
    lo = (x - hi.astype(F32)).astype(BF16)
    return hi, lo


def _router_body(x_ref, nw_ref, wr_ref, xn_ref, info_ref, cnt_ref):
    tm = x_ref.shape[0]

    @pl.when(pl.program_id(0) == 0)
    def _():
        cnt_ref[...] = jnp.zeros(cnt_ref.shape, F32)

    x = x_ref[...]
    xn = x * lax.rsqrt(jnp.mean(x * x, -1, keepdims=True) + EPS) * nw_ref[...]
    _store_token_tiles(xn_ref, tm, xn)
    x_hi, x_lo = _split2(xn)
    logits = (jnp.dot(x_lo, wr_ref[0], preferred_element_type=F32)
              + jnp.dot(x_hi, wr_ref[1], preferred_element_type=F32)
              + jnp.dot(x_hi, wr_ref[0], preferred_element_type=F32))

    lane = lax.broadcasted_iota(I32, (tm, LANES), 1)
    neg = jnp.float32(-jnp.inf)
    big = jnp.int32(2 * LANES)
    is_grp = (lane >= GRP_LANE0) & (lane < GRP_LANE0 + MOE_GROUPS)
    lg = jnp.where(is_grp, logits, neg)
    mg = jnp.max(lg, -1, keepdims=True)
    grp = jnp.min(jnp.where(lg == mg, lane, big), -1, keepdims=True) - GRP_LANE0
    p_sel = 1.0 / jnp.sum(jnp.where(is_grp, jnp.exp(lg - mg), 0.0), -1, keepdims=True)

    in_grp = (lane < N_EXPERTS) & ((lane // MOE_PER_GROUP) == grp)
    le = jnp.where(in_grp, logits, neg)
    m0 = jnp.max(le, -1, keepdims=True)
    e0 = jnp.min(jnp.where(le == m0, lane, big), -1, keepdims=True)
    le1 = jnp.where(lane == e0, neg, le)
    m1 = jnp.max(le1, -1, keepdims=True)
    e1 = jnp.min(jnp.where(le1 == m1, lane, big), -1, keepdims=True)
    r = jnp.exp(m1 - m0)
    w0 = p_sel / (1.0 + r)
    w1 = p_sel * r / (1.0 + r)

    hot0 = lane == e0
    hot1 = lane == e1
    onehot = jnp.where(hot0 | hot1, 1.0, 0.0)
    ti = lax.broadcasted_iota(I32, (tm, tm), 0)
    tj = lax.broadcasted_iota(I32, (tm, tm), 1)
    before = jnp.where(ti > tj, 1.0, 0.0).astype(BF16)
    base = jnp.dot(before, onehot.astype(BF16), preferred_element_type=F32) + cnt_ref[...]
    r0 = jnp.sum(jnp.where(hot0, base, 0.0), -1, keepdims=True)
    r1 = jnp.sum(jnp.where(hot1, base, 0.0), -1, keepdims=True)
    cnt_ref[...] = cnt_ref[...] + jnp.sum(onehot, 0, keepdims=True)

    info = jnp.zeros((tm, LANES), F32)
    for idx, val in ((INFO_E0, e0.astype(F32)), (INFO_E1, e1.astype(F32)), (INFO_W0, w0),
                     (INFO_W1, w1), (INFO_R0, r0), (INFO_R1, r1)):
        info = jnp.where(lane == idx, val, info)
    info_ref[...] = info


def _router(x2d, norm_w, router_grp, router_exp):
    n, d = x2d.shape
    tm = min(ROW_TILE, n)
    w_r = jnp.zeros((d, LANES), F32).at[:, :N_EXPERTS].set(router_exp)
    w_r = w_r.at[:, GRP_LANE0:GRP_LANE0 + MOE_GROUPS].set(router_grp)
    w_parts = jnp.stack(_split2(w_r))
    return pl.pallas_call(
        _router_body,
        grid=(n // tm,),
        in_specs=[pl.BlockSpec((tm, d), lambda i: (i, 0)), _full((1, d)), _full(w_parts.shape)],
        out_specs=[pl.BlockSpec((tm * ROW_PIECES, LANES), lambda i: (i, 0)),
                   pl.BlockSpec((tm, LANES), lambda i: (i, 0)),
                   _full((1, LANES))],
        out_shape=[jax.ShapeDtypeStruct((n * ROW_PIECES, LANES), F32),
                   jax.ShapeDtypeStruct((n, LANES), F32),
                   jax.ShapeDtypeStruct((1, LANES), F32)],
        compiler_params=_params(("arbitrary",)),
        name="router",
    )(x2d, norm_w.reshape(1, d), w_parts)


ROW_PIECES = D_MODEL // LANES
assert ROW_PIECES == SUBLANES


def _store_token_tiles(ref, rows, value):
    for s in range(ROW_PIECES):
        ref[pl.ds(s, rows, stride=ROW_PIECES), :] = value[:, s * LANES:(s + 1) * LANES]


def _load_token_piece(ref, rows, s):
    return ref[pl.ds(s, rows, stride=ROW_PIECES), :]


RANK_BITS = 16
SLOT_FREE = -1
INVERT_UNROLL = 8


INVERT_STEP = 1024


def _invert_body(code_ref, start_ref, free_hbm, slot_ref, sem):
    step = pl.program_id(0)

    @pl.when(step == 0)
    def _():
        fill = pltpu.make_async_copy(free_hbm, slot_ref, sem)
        fill.start()
        fill.wait()

    def body(i, carry):
        code = code_ref[i]
        expert = lax.shift_right_logical(code, RANK_BITS)
        rank = code & ((1 << RANK_BITS) - 1)
        slot_ref[start_ref[expert] + rank] = step * INVERT_STEP + i
        return carry

    lax.fori_loop(0, INVERT_STEP, body, 0, unroll=INVERT_UNROLL)


def _invert(code, pad_start, n_rows):
    smem = pl.BlockSpec(memory_space=pltpu.SMEM)
    return pl.pallas_call(
        _invert_body,
        grid=(code.shape[0] // INVERT_STEP,),
        in_specs=[pl.BlockSpec((INVERT_STEP,), lambda i: (i,), memory_space=pltpu.SMEM), smem,
                  pl.BlockSpec(memory_space=pl.ANY)],
        out_specs=smem,
        out_shape=jax.ShapeDtypeStruct((n_rows,), I32),
        scratch_shapes=[pltpu.SemaphoreType.DMA(())],
        compiler_params=_params(("arbitrary",)),
        name="invert",
    )(code, pad_start, jnp.full((n_rows,), SLOT_FREE, I32))


def _row_copies(idx_ref, hbm, buf, sem, to_vmem):
    copies = []
    for r in range(MOE_BLOCK):
        hbm_row = hbm.at[pl.ds(pl.multiple_of(idx_ref[0, r], ROW_PIECES), ROW_PIECES)]
        buf_row = buf.at[pl.ds(r * ROW_PIECES, ROW_PIECES)]
        src, dst = (hbm_row, buf_row) if to_vmem else (buf_row, hbm_row)
        copies.append(pltpu.make_async_copy(src, dst, sem))
    return copies


DMA_THREADS = 2


def _start_all(copies):
    for r, cp in enumerate(copies):
        cp.start(priority=r % DMA_THREADS)


def _ffn_body(blk_ref, last_ref, gcur_ref, gnext_ref, sprev_ref, scur_ref, x_hbm,
              wg_ref, wu_ref, wd_ref, yt_hbm, xbuf, ybuf, wg16, wu16, wd16, gsem, ssem):
    b = pl.program_id(0)
    last = last_ref[0]

    @pl.when(b <= last)
    def _():
        slot = b % 2
        other = 1 - slot

        @pl.when(b == 0)
        def _():
            ybuf[...] = jnp.zeros(ybuf.shape, F32)
            _start_all(_row_copies(gcur_ref, x_hbm, xbuf.at[0], gsem.at[0], True))

        @pl.when(b > 0)
        def _():
            for cp in _row_copies(sprev_ref, yt_hbm, ybuf.at[slot], ssem.at[slot], False):
                cp.wait()

        for cp in _row_copies(gcur_ref, x_hbm, xbuf.at[slot], gsem.at[slot], True):
            cp.wait()

        @pl.when((b == 0) | (blk_ref[b] != blk_ref[jnp.maximum(b - 1, 0)]))
        def _():
            wg16[...] = wg_ref[...].astype(BF16)
            wu16[...] = wu_ref[...].astype(BF16)
            wd16[...] = wd_ref[...].astype(BF16)

        _start_all(_row_copies(gnext_ref, x_hbm, xbuf.at[other], gsem.at[other], True))
        _start_all(_row_copies(scur_ref, yt_hbm, ybuf.at[other], ssem.at[other], False))
        xb = jnp.concatenate([_load_token_piece(xbuf.at[slot], MOE_BLOCK, s).astype(BF16)
                              for s in range(ROW_PIECES)], axis=1)
        g = jnp.dot(xb, wg16[...], preferred_element_type=F32)
        u = jnp.dot(xb, wu16[...], preferred_element_type=F32)
        hb = (_silu(g) * u).astype(BF16)
        _store_token_tiles(ybuf.at[slot], MOE_BLOCK, jnp.dot(hb, wd16[...], preferred_element_type=F32))

        @pl.when(b == last)
        def _():
            for cp in _row_copies(gnext_ref, x_hbm, xbuf.at[other], gsem.at[other], True):
                cp.wait()
            for cp in _row_copies(scur_ref, yt_hbm, ybuf.at[other], ssem.at[other], False):
                cp.wait()


def _ffn(xn2d, gather_rows, scatter_rows, blk_expert, last_step, n_out_rows, layer,
         w_gate, w_up, w_down):
    d = D_MODEL
    de = w_gate.shape[-1]
    n_steps = blk_expert.shape[0]
    tile_buf = pltpu.VMEM((2, MOE_BLOCK * ROW_PIECES, LANES), F32)
    idx_spec = lambda shift: pl.BlockSpec(
        (None, 1, MOE_BLOCK), lambda b, e, u: (jnp.maximum(b + shift, 0), 0, 0),
        memory_space=pltpu.SMEM)
    w_spec = lambda rows, cols: pl.BlockSpec((None, None, rows, cols),
                                             lambda b, e, u: (layer, e[b], 0, 0))
    grid_spec = pltpu.PrefetchScalarGridSpec(
        num_scalar_prefetch=2,
        grid=(n_steps,),
        in_specs=[idx_spec(0), idx_spec(1), idx_spec(-1), idx_spec(0),
                  pl.BlockSpec(memory_space=pl.ANY),
                  w_spec(d, de), w_spec(d, de), w_spec(de, d)],
        out_specs=pl.BlockSpec(memory_space=pl.ANY),
        scratch_shapes=[tile_buf, tile_buf,
                        pltpu.VMEM((d, de), BF16), pltpu.VMEM((d, de), BF16),
                        pltpu.VMEM((de, d), BF16),
                        pltpu.SemaphoreType.DMA((2,)), pltpu.SemaphoreType.DMA((2,))],
    )
    return pl.pallas_call(
        _ffn_body,
        grid_spec=grid_spec,
        out_shape=jax.ShapeDtypeStruct((n_out_rows * ROW_PIECES, LANES), F32),
        compiler_params=_params(("arbitrary",)),
        name="expert_ffn",
    )(blk_expert, last_step, gather_rows, gather_rows, scatter_rows, scatter_rows, xn2d,
      w_gate, w_up, w_down)


def _combine_body(x_ref, y0_ref, y1_ref, info_ref, nw_ref, *out_refs):
    tm = x_ref.shape[0]
    info = info_ref[...]
    w0 = info[:, INFO_W0:INFO_W0 + 1]
    w1 = info[:, INFO_W1:INFO_W1 + 1]
    x = jnp.concatenate(
        [x_ref[:, s * LANES:(s + 1) * LANES] + w0 * _load_token_piece(y0_ref, tm, s)
         + w1 * _load_token_piece(y1_ref, tm, s) for s in range(ROW_PIECES)], axis=1)
    xn = x * lax.rsqrt(jnp.mean(x * x, -1, keepdims=True) + EPS) * nw_ref[...]
    if len(out_refs) == 2:
        out_refs[0][...] = x
    out_refs[-1][...] = xn.astype(out_refs[-1].dtype)


def _combine(x2d, yt, info, next_norm_w, final):
    n, d = x2d.shape
    tm = MOE_BLOCK
    row = lambda width: pl.BlockSpec((tm, width), lambda i: (i, 0))
    slot0 = pl.BlockSpec((tm * ROW_PIECES, LANES), lambda i: (i, 0))
    slot1 = pl.BlockSpec((tm * ROW_PIECES, LANES), lambda i: (i + n // tm + 1, 0))
    out_shape = [jax.ShapeDtypeStruct((n, d), F32)]
    if not final:
        out_shape.append(jax.ShapeDtypeStruct((n, d), BF16))
    return pl.pallas_call(
        _combine_body,
        grid=(n // tm,),
        in_specs=[row(d), slot0, slot1, row(LANES), _full((1, d))],
        out_specs=[row(d)] * len(out_shape),
        out_shape=out_shape,
        compiler_params=_params(("parallel",)),
        name="combine",
    )(x2d, yt, yt, info, next_norm_w.reshape(1, d))


def _hier_moe(x2d, norm_w, router_grp, router_exp, layer, w_gate, w_up, w_down, next_norm_w, final):
    n, _ = x2d.shape
    xn, info, cnt = _router(x2d, norm_w, router_grp, router_exp)
    counts = cnt[0, :N_EXPERTS].astype(I32)
    padded = (counts + MOE_BLOCK - 1) // MOE_BLOCK * MOE_BLOCK
    pad_end = jnp.cumsum(padded)
    pad_start = pad_end - padded
    expert = info[:, INFO_E0:INFO_E1 + 1].astype(I32)
    rank = info[:, INFO_R0:INFO_R1 + 1].astype(I32)
    code = ((expert << RANK_BITS) | rank).reshape(-1)
    n_blocks = (2 * n + N_EXPERTS * (MOE_BLOCK - 1)) // MOE_BLOCK + 1
    slots = _invert(code, pad_start, n_blocks * MOE_BLOCK)
    free = slots < 0
    spare = n + jnp.arange(MOE_BLOCK, dtype=I32)
    gather_rows = jnp.where(free, 0, slots >> 1)
    scatter_rows = jnp.where(free, jnp.tile(spare, n_blocks),
                             (slots & 1) * (n + MOE_BLOCK) + (slots >> 1))
    gather_rows = jnp.concatenate([gather_rows, jnp.zeros((2 * MOE_BLOCK,), I32)])
    scatter_rows = jnp.concatenate([spare, scatter_rows, spare])
    blk_start = jnp.arange(n_blocks + 1, dtype=I32) * MOE_BLOCK
    blk_expert = jnp.minimum(jnp.sum((pad_end[None, :] <= blk_start[:, None]).astype(I32), axis=1),
                             N_EXPERTS - 1)
    last_step = (pad_end[-1:] // MOE_BLOCK).astype(I32)
    yt = _ffn(xn, (gather_rows * ROW_PIECES).reshape(n_blocks + 2, 1, MOE_BLOCK),
              (scatter_rows * ROW_PIECES).reshape(n_blocks + 2, 1, MOE_BLOCK), blk_expert, last_step,
              2 * n + MOE_BLOCK, layer, w_gate, w_up, w_down)
    return _combine(x2d, yt, info, next_norm_w, final)


def _mixer_block(x2d, xn2d, bsz, w_in, gdn_conv_w, gdn_a_log, gdn_dt_bias, gdn_norm_w,
                 conf_conv_w, conf_conv_b, conf_ln_w, conf_ln_b, gmlp_ln_w, gmlp_ln_b,
                 gmlp_w_s, gmlp_b_s, sc_conv_w, w_branch, w_out):
    n, d = x2d.shape
    t_len = n // bsz
    xn = xn2d.reshape(bsz, t_len, d)
    c0 = 2 * GDN_QK + 2 * GDN_V
    c1 = c0 + GDN_HEADS
    c2 = c1 + GDN_HEADS
    c3 = c2 + 2 * BRANCH_WIDTH
    c4 = c3 + 2 * BRANCH_WIDTH
    c5 = c4 + 3 * BRANCH_WIDTH
    w16 = w_in.astype(BF16)
    ys = (
        _gdn(xn, w16[:, :c0], w16[:, c0:c1], w16[:, c1:c2], gdn_conv_w, gdn_a_log, gdn_dt_bias,
             gdn_norm_w),
        _conformer(xn, w16[:, c2:c3], conf_conv_w, conf_conv_b, conf_ln_w, conf_ln_b),
        _gmlp(xn, w16[:, c3:c4], gmlp_ln_w, gmlp_ln_b, gmlp_w_s, gmlp_b_s),
        _shortconv(xn, w16[:, c4:c5], sc_conv_w),
    )
    ys = [y.reshape(bsz * t_len, BRANCH_WIDTH) for y in ys]
    w_gate = w16[:, c5:].reshape(d, N_BRANCH, d).transpose(1, 0, 2)
    return _merge(x2d, xn2d, ys, w_gate, w_branch.astype(BF16), w_out.astype(BF16))


def kernel(x, mix_norm_w, w_in, gdn_conv_w, gdn_A_log, gdn_dt_bias, gdn_norm_w, conf_conv_w,
           conf_conv_b, conf_ln_w, conf_ln_b, gmlp_ln_w, gmlp_ln_b, gmlp_w_s, gmlp_b_s, sc_conv_w,
           w_branch, w_out, moe_norm_w, router_grp, router_exp, w_gate, w_up, w_down, final_norm_w):
    bsz, t_len, d = x.shape
    depth = mix_norm_w.shape[0]
    x2d = x.reshape(bsz * t_len, d)
    xn2d = _rmsnorm(x2d, mix_norm_w[0], BF16)
    for l in range(depth):
        x2d = _mixer_block(x2d, xn2d, bsz, w_in[l], gdn_conv_w[l], gdn_A_log[l], gdn_dt_bias[l],
                           gdn_norm_w[l], conf_conv_w[l], conf_conv_b[l], conf_ln_w[l],
                           conf_ln_b[l], gmlp_ln_w[l], gmlp_ln_b[l], gmlp_w_s[l], gmlp_b_s[l],
                           sc_conv_w[l], w_branch[l], w_out[l])
        final = l == depth - 1
        next_norm_w = final_norm_w if final else mix_norm_w[l + 1]
        outs = _hier_moe(x2d, moe_norm_w[l], router_grp[l], router_exp[l], l, w_gate, w_up, w_down,
                         next_norm_w, final)
        if final:
            return outs[0].reshape(bsz, t_len, d)
        x2d, xn2d = outs
```

```python
import functools

import jax
import jax.numpy as jnp
from jax import lax
from jax.experimental import pallas as pl
from jax.experimental.pallas import tpu as pltpu

F32 = jnp.float32
BF16 = jnp.bfloat16
I32 = jnp.int32

D_MODEL = 1024
BRANCH_WIDTH = 512
N_BRANCH = 4
GDN_HEADS = 4
GDN_DK = 128
GDN_DV = 128
GDN_CONV = 4
GDN_CHUNK = 64
GDN_QK = GDN_HEADS * GDN_DK
GDN_V = GDN_HEADS * GDN_DV
CONF_KERNEL = 31
GMLP_GROUPS = 4
GMLP_CHUNK = 128
SC_KERNEL = 3
MOE_GROUPS = 8
MOE_PER_GROUP = 8
N_EXPERTS = MOE_GROUPS * MOE_PER_GROUP
D_EXPERT = 512
MOE_BLOCK = 256
EPS = 1e-6

LANES = 128
SUBLANES = 8
VMEM_LIMIT = 56 * 1024 * 1024

SEQ_TILE = 512
ROW_TILE = 512
CONF_HALO = 32
SMALL_HALO = SUBLANES
CONV_ROWS = 64

INFO_E0, INFO_E1, INFO_W0, INFO_W1, INFO_R0, INFO_R1 = range(6)
GRP_LANE0 = N_EXPERTS


def _params(sem):
    return pltpu.CompilerParams(dimension_semantics=sem, vmem_limit_bytes=VMEM_LIMIT)


def _full(shape):
    nd = len(shape)
    return pl.BlockSpec(shape, lambda *_: (0,) * nd)


def _bdot(a, b):
    return jnp.dot(a.astype(BF16), b.astype(BF16), preferred_element_type=F32)


def _sigmoid(x):
    return 1.0 / (1.0 + jnp.exp(-x))


def _silu(x):
    return x * _sigmoid(x)


def _rms_body(x_ref, w_ref, o_ref):
    x = x_ref[...]
    ms = jnp.mean(x * x, -1, keepdims=True)
    o_ref[...] = (x * lax.rsqrt(ms + EPS) * w_ref[...]).astype(o_ref.dtype)


def _rmsnorm(x2d, w, out_dtype):
    n, d = x2d.shape
    tm = min(ROW_TILE, n)
    return pl.pallas_call(
        _rms_body,
        grid=(n // tm,),
        in_specs=[pl.BlockSpec((tm, d), lambda i: (i, 0)), _full((1, d))],
        out_specs=pl.BlockSpec((tm, d), lambda i: (i, 0)),
        out_shape=jax.ShapeDtypeStruct((n, d), out_dtype),
        compiler_params=_params(("parallel",)),
        name="rmsnorm",
    )(x2d, w.reshape(1, d))


def _carry_halo(buf, halo, tile):
    t = pl.program_id(1)

    @pl.when(t == 0)
    def _():
        buf[0:halo, :] = jnp.zeros((halo, buf.shape[1]), buf.dtype)

    @pl.when(t != 0)
    def _():
        buf[0:halo, :] = buf[tile:tile + halo, :]


def _causal_taps(buf, cw_ref, halo, row0, rows, init):
    k_w = cw_ref.shape[0]
    acc = init
    for k in range(k_w):
        off = halo - (k_w - 1) + k + row0
        acc = acc + cw_ref[k:k + 1, :] * buf[off:off + rows, :]
    return acc


def _causal_taps_by_phase(buf, cw_ref, halo, row0, rows, init):
    k_w = cw_ref.shape[0]
    base = halo - (k_w - 1)
    acc = init
    for phase in range(SUBLANES):
        taps = [k for k in range(k_w) if (base + k) % SUBLANES == phase]
        span = rows + (SUBLANES if phase else 0)
        part = None
        for k in taps:
            off = base + k - phase + row0
            term = cw_ref[k:k + 1, :] * buf[off:off + span, :]
            part = term if part is None else part + term
        if part is not None:
            acc = acc + part[phase:phase + rows, :]
    return acc


def _layernorm(x, w, b):
    mu = jnp.mean(x, -1, keepdims=True)
    xc = x - mu
    var = jnp.mean(xc * xc, -1, keepdims=True)
    return xc * lax.rsqrt(var + EPS) * w + b


def _seq_call(body, xn, weights, out_width, scratch, name):
    bsz, t_len, d = xn.shape
    tile = min(SEQ_TILE, t_len)
    in_specs = [pl.BlockSpec((None, tile, d), lambda b, t: (b, t, 0))]
    in_specs += [_full(w.shape) for w in weights]
    return pl.pallas_call(
        body,
        grid=(bsz, t_len // tile),
        in_specs=in_specs,
        out_specs=pl.BlockSpec((None, tile, out_width), lambda b, t: (b, t, 0)),
        out_shape=jax.ShapeDtypeStruct((bsz, t_len, out_width), BF16),
        scratch_shapes=scratch(tile),
        compiler_params=_params(("parallel", "arbitrary")),
        name=name,
    )(xn, *weights)


def _conf_body(xn_ref, w_ref, cw_ref, cb_ref, lw_ref, lb_ref, o_ref, ybuf):
    tile, width = o_ref.shape
    _carry_halo(ybuf, CONF_HALO, tile)
    h = jnp.dot(xn_ref[...], w_ref[...], preferred_element_type=F32)
    ybuf[CONF_HALO:CONF_HALO + tile, :] = h[:, :width] * _sigmoid(h[:, width:])
    for r in range(tile // CONV_ROWS):
        init = jnp.broadcast_to(cb_ref[...], (CONV_ROWS, width))
        y = _causal_taps_by_phase(ybuf, cw_ref, CONF_HALO, r * CONV_ROWS, CONV_ROWS, init)
        y = _layernorm(y, lw_ref[...], lb_ref[...])
        o_ref[r * CONV_ROWS:(r + 1) * CONV_ROWS, :] = _silu(y).astype(o_ref.dtype)


def _conformer(xn, w, conv_w, conv_b, ln_w, ln_b):
    width = BRANCH_WIDTH
    weights = [w, conv_w, conv_b.reshape(1, width), ln_w.reshape(1, width), ln_b.reshape(1, width)]
    scratch = lambda tile: [pltpu.VMEM((CONF_HALO + tile, width), F32)]
    return _seq_call(_conf_body, xn, weights, width, scratch, "conformer")


def _sc_body(xn_ref, w_ref, cw_ref, o_ref, ubuf, bbuf):
    tile, width = o_ref.shape
    _carry_halo(ubuf, SMALL_HALO, tile)
    h = jnp.dot(xn_ref[...], w_ref[...], preferred_element_type=F32)
    bbuf[...] = h[:, :width]
    ubuf[SMALL_HALO:SMALL_HALO + tile, :] = h[:, width:2 * width] * h[:, 2 * width:]
    for r in range(tile // CONV_ROWS):
        rows = slice(r * CONV_ROWS, (r + 1) * CONV_ROWS)
        init = jnp.zeros((CONV_ROWS, width), F32)
        y = _causal_taps(ubuf, cw_ref, SMALL_HALO, r * CONV_ROWS, CONV_ROWS, init)
        o_ref[rows, :] = (bbuf[rows, :] * y).astype(o_ref.dtype)


def _shortconv(xn, w, conv_w):
    width = BRANCH_WIDTH
    scratch = lambda tile: [pltpu.VMEM((SMALL_HALO + tile, width), F32), pltpu.VMEM((tile, width), F32)]
    return _seq_call(_sc_body, xn, [w, conv_w], width, scratch, "shortconv")


def _gmlp_body(xn_ref, w_ref, lw_ref, lb_ref, ws_ref, bs_ref, o_ref, ubuf, vbuf):
    tile, width = o_ref.shape
    gc = width // GMLP_GROUPS
    h = jnp.dot(xn_ref[...], w_ref[...], preferred_element_type=F32)
    g = jax.nn.gelu(h)
    ubuf[...] = g[:, :width]
    vbuf[...] = _layernorm(g[:, width:], lw_ref[...], lb_ref[...]).astype(BF16)
    row = lax.broadcasted_iota(I32, (GMLP_CHUNK, GMLP_CHUNK), 0)
    col = lax.broadcasted_iota(I32, (GMLP_CHUNK, GMLP_CHUNK), 1)
    for gi in range(GMLP_GROUPS):
        w_c = jnp.where(row >= col, ws_ref[gi], 0.0).astype(BF16)
        cols = slice(gi * gc, (gi + 1) * gc)
        for c in range(tile // GMLP_CHUNK):
            rows = slice(c * GMLP_CHUNK, (c + 1) * GMLP_CHUNK)
            mixed = jnp.dot(w_c, vbuf[rows, cols], preferred_element_type=F32) + bs_ref[gi]
            o_ref[rows, cols] = (ubuf[rows, cols] * mixed).astype(o_ref.dtype)


def _gmlp(xn, w, ln_w, ln_b, w_s, b_s):
    width = BRANCH_WIDTH
    gc = width // GMLP_GROUPS
    b_full = jnp.broadcast_to(b_s[:, :, None], (GMLP_GROUPS, GMLP_CHUNK, gc))
    weights = [w, ln_w.reshape(1, width), ln_b.reshape(1, width), w_s, b_full]
    scratch = lambda tile: [pltpu.VMEM((tile, width), F32), pltpu.VMEM((tile, width), BF16)]
    return _seq_call(_gmlp_body, xn, weights, width, scratch, "gmlp")


def _softplus(x):
    return jnp.maximum(x, 0.0) + jnp.log(1.0 + jnp.exp(-jnp.abs(x)))


def _bmm(a, b):
    return jnp.einsum("nij,njk->nik", a.astype(BF16), b.astype(BF16), preferred_element_type=F32)


def _bmm_nt(a, b):
    return jnp.einsum("nid,njd->nij", a.astype(BF16), b.astype(BF16), preferred_element_type=F32)


def _bmm_tn(a, b):
    return jnp.einsum("ncd,nce->nde", a.astype(BF16), b.astype(BF16), preferred_element_type=F32)


def _unit_lower_inverse(strict_lower, eye):
    size = strict_lower.shape[-1]
    inv = eye - strict_lower
    power = strict_lower
    span = 2
    while span < size:
        power = _bmm(power, power)
        inv = inv + _bmm(inv, power)
        span *= 2
    return inv


def _split3(x):
    hi = x.astype(BF16)
    r1 = x - hi.astype(F32)
    mid = r1.astype(BF16)
    lo = (r1 - mid.astype(F32)).astype(BF16)
    return hi, mid, lo


def _gdn_body(xn_ref, w_ref, wb_ref, wa_ref, cw_ref, alog_ref, dtb_ref, nw_ref, o_ref,
              hbuf, qbuf, kbuf, vbuf, zbuf, gbuf, bbuf, s_ref, wc_ref, bc_ref, qp_ref, op_ref, egl_ref):
    tile = o_ref.shape[0]
    chunk = GDN_CHUNK
    t = pl.program_id(1)
    _carry_halo(hbuf, SMALL_HALO, tile)

    @pl.when(t == 0)
    def _():
        s_ref[...] = jnp.zeros(s_ref.shape, F32)

    xn = xn_ref[...]
    h = jnp.dot(xn, w_ref[...], preferred_element_type=F32)
    n_conv = 2 * GDN_QK + GDN_V
    hbuf[SMALL_HALO:SMALL_HALO + tile, :] = h[:, :n_conv]
    zbuf[...] = h[:, n_conv:]
    bbuf[...] = _sigmoid(jnp.dot(xn, wb_ref[...], preferred_element_type=F32))
    a_raw = jnp.dot(xn, wa_ref[...], preferred_element_type=F32)
    gbuf[...] = -jnp.exp(alog_ref[...]) * _softplus(a_raw + dtb_ref[...])

    for r in range(tile // CONV_ROWS):
        rows = slice(r * CONV_ROWS, (r + 1) * CONV_ROWS)
        init = jnp.zeros((CONV_ROWS, n_conv), F32)
        qkv = _silu(_causal_taps(hbuf, cw_ref, SMALL_HALO, r * CONV_ROWS, CONV_ROWS, init))
        for hd in range(GDN_HEADS):
            q = qkv[:, hd * GDN_DK:(hd + 1) * GDN_DK]
            k = qkv[:, GDN_QK + hd * GDN_DK:GDN_QK + (hd + 1) * GDN_DK]
            q = q * lax.rsqrt(jnp.sum(q * q, -1, keepdims=True) + EPS) * (GDN_DK ** -0.5)
            k = k * lax.rsqrt(jnp.sum(k * k, -1, keepdims=True) + EPS)
            qbuf[rows, hd * GDN_DK:(hd + 1) * GDN_DK] = q
            kbuf[rows, hd * GDN_DK:(hd + 1) * GDN_DK] = k
        vbuf[rows, :] = qkv[:, 2 * GDN_QK:]

    n_chunks = tile // chunk
    n_batch = GDN_HEADS * n_chunks
    row = lax.broadcasted_iota(I32, (chunk, chunk), 0)
    col = lax.broadcasted_iota(I32, (chunk, chunk), 1)
    tril = row >= col
    strict = row > col
    eye = jnp.where(row == col, 1.0, 0.0)

    def per_head(buf, width):
        return jnp.concatenate(
            [buf[:, hd * width:(hd + 1) * width].reshape(n_chunks, chunk, width)
             for hd in range(GDN_HEADS)], axis=0)

    g_lo, g_mid, g_hi = reversed(_split3(gbuf[...].reshape(n_chunks, chunk, LANES)))
    ones_tril = jnp.broadcast_to(jnp.where(tril, 1.0, 0.0).astype(BF16), (n_chunks, chunk, chunk))
    g_cum = _bmm(ones_tril, g_lo) + _bmm(ones_tril, g_mid) + _bmm(ones_tril, g_hi)
    g_cum_t = [g_cum[c].T for c in range(n_chunks)]
    g_col = jnp.concatenate([g_cum[:, :, hd:hd + 1] for hd in range(GDN_HEADS)], axis=0)
    g_row = jnp.concatenate(
        [g_cum_t[c][hd:hd + 1, :][None] for hd in range(GDN_HEADS) for c in range(n_chunks)], axis=0)
    b3 = bbuf[...].reshape(n_chunks, chunk, LANES)
    b_col = jnp.concatenate([b3[:, :, hd:hd + 1] for hd in range(GDN_HEADS)], axis=0)
    g_last = g_col[:, chunk - 1:chunk, :]

    q = per_head(qbuf[...], GDN_DK)
    k = per_head(kbuf[...], GDN_DK)
    v = per_head(vbuf[...], GDN_DV)
    decay = jnp.where(tril, jnp.exp(jnp.where(tril, g_col - g_row, 0.0)), 0.0)
    e_g = jnp.exp(g_col)
    k_beta = k * b_col
    inv = _unit_lower_inverse(jnp.where(strict, _bmm_nt(k_beta, k) * decay, 0.0), eye)
    uw = _bmm(inv, jnp.concatenate([v * b_col, k_beta * e_g], axis=2))
    u = uw[:, :, :GDN_DV]
    w = uw[:, :, GDN_DV:]
    attn = _bmm_nt(q, k) * decay
    k_dec = k * jnp.exp(g_last - g_col)
    wc_ref[...] = _bmm_tn(k_dec, w).astype(BF16)
    bc_ref[...] = _bmm_tn(k_dec, u)
    qp_ref[...] = (q * e_g - _bmm(attn, w)).astype(BF16)
    op_ref[...] = _bmm(attn, u)
    egl_ref[...] = jnp.broadcast_to(jnp.exp(g_last), (n_batch, 1, GDN_DV))

    for c in range(n_chunks):
        rows = slice(c * chunk, (c + 1) * chunk)
        for hd in range(GDN_HEADS):
            n = hd * n_chunks + c
            cols = slice(hd * GDN_DV, (hd + 1) * GDN_DV)
            state = s_ref[hd]
            s16 = state.astype(BF16)
            o = jnp.dot(qp_ref[n], s16, preferred_element_type=F32) + op_ref[n]
            s_ref[hd] = (state * egl_ref[n] + bc_ref[n]
                         - jnp.dot(wc_ref[n], s16, preferred_element_type=F32))
            o = o * lax.rsqrt(jnp.mean(o * o, -1, keepdims=True) + EPS) * nw_ref[...]
            o_ref[rows, cols] = (o * _silu(zbuf[rows, cols])).astype(o_ref.dtype)


def _lane_pad(vec):
    return jnp.zeros((1, LANES), F32).at[0, :vec.shape[0]].set(vec)


def _gdn(xn, w, w_beta, w_a, conv_w, a_log, dt_bias, norm_w):
    pad = ((0, 0), (0, LANES - GDN_HEADS))
    weights = [w, jnp.pad(w_beta, pad), jnp.pad(w_a, pad), conv_w,
               _lane_pad(a_log), _lane_pad(dt_bias), norm_w.reshape(1, GDN_DV)]
    n_conv = 2 * GDN_QK + GDN_V

    def scratch(tile):
        n_batch = GDN_HEADS * (tile // GDN_CHUNK)
        return [pltpu.VMEM((SMALL_HALO + tile, n_conv), F32),
                pltpu.VMEM((tile, GDN_QK), F32), pltpu.VMEM((tile, GDN_QK), F32),
                pltpu.VMEM((tile, GDN_V), F32), pltpu.VMEM((tile, GDN_V), F32),
                pltpu.VMEM((tile, LANES), F32), pltpu.VMEM((tile, LANES), F32),
                pltpu.VMEM((GDN_HEADS, GDN_DK, GDN_DV), F32),
                pltpu.VMEM((n_batch, GDN_DK, GDN_DK), BF16),
                pltpu.VMEM((n_batch, GDN_DK, GDN_DV), F32),
                pltpu.VMEM((n_batch, GDN_CHUNK, GDN_DK), BF16),
                pltpu.VMEM((n_batch, GDN_CHUNK, GDN_DV), F32),
                pltpu.VMEM((n_batch, 1, GDN_DV), F32)]

    return _seq_call(_gdn_body, xn, weights, GDN_V, scratch, "gdn")


def _merge_body(x_ref, xn_ref, y0_ref, y1_ref, y2_ref, y3_ref, wg_ref, wb_ref, wo_ref, o_ref):
    xn = xn_ref[...]
    merged = None
    for i, y_ref in enumerate((y0_ref, y1_ref, y2_ref, y3_ref)):
        gate = _sigmoid(jnp.dot(xn, wg_ref[i], preferred_element_type=F32))
        term = gate * jnp.dot(y_ref[...], wb_ref[i], preferred_element_type=F32)
        merged = term if merged is None else merged + term
    o_ref[...] = x_ref[...] + jnp.dot(merged.astype(BF16), wo_ref[...], preferred_element_type=F32)


def _merge(x2d, xn2d, ys, w_gate, w_branch, w_out):
    n, d = x2d.shape
    tm = min(ROW_TILE, n)
    row = lambda width: pl.BlockSpec((tm, width), lambda i: (i, 0))
    return pl.pallas_call(
        _merge_body,
        grid=(n // tm,),
        in_specs=[row(d), row(d)] + [row(BRANCH_WIDTH)] * N_BRANCH
        + [_full(w_gate.shape), _full(w_branch.shape), _full(w_out.shape)],
        out_specs=row(d),
        out_shape=jax.ShapeDtypeStruct((n, d), F32),
        compiler_params=_params(("parallel",)),
        name="merge",
    )(x2d, xn2d, *ys, w_gate, w_branch, w_out)


def _split2(x):
    hi = x.astype(BF16)
    lo = (x - hi.astype(F32)).astype(BF16)
    return hi, lo


def _router_body(x_ref, nw_ref, wr_ref, xn_ref, info_ref, cnt_ref):
    tm = x_ref.shape[0]

    @pl.when(pl.program_id(0) == 0)
    def _():
        cnt_ref[...] = jnp.zeros(cnt_ref.shape, F32)

    x = x_ref[...]
    xn = x * lax.rsqrt(jnp.mean(x * x, -1, keepdims=True) + EPS) * nw_ref[...]
    _store_token_tiles(xn_ref, tm, xn)
    x_hi, x_lo = _split2(xn)
    logits = (jnp.dot(x_lo, wr_ref[0], preferred_element_type=F32)
              + jnp.dot(x_hi, wr_ref[1], preferred_element_type=F32)
              + jnp.dot(x_hi, wr_ref[0], preferred_element_type=F32))

    lane = lax.broadcasted_iota(I32, (tm, LANES), 1)
    neg = jnp.float32(-jnp.inf)
    big = jnp.int32(2 * LANES)
    is_grp = (lane >= GRP_LANE0) & (lane < GRP_LANE0 + MOE_GROUPS)
    lg = jnp.where(is_grp, logits, neg)
    mg = jnp.max(lg, -1, keepdims=True)
    grp = jnp.min(jnp.where(lg == mg, lane, big), -1, keepdims=True) - GRP_LANE0
    p_sel = 1.0 / jnp.sum(jnp.where(is_grp, jnp.exp(lg - mg), 0.0), -1, keepdims=True)

    in_grp = (lane < N_EXPERTS) & ((lane // MOE_PER_GROUP) == grp)
    le = jnp.where(in_grp, logits, neg)
    m0 = jnp.max(le, -1, keepdims=True)
    e0 = jnp.min(jnp.where(le == m0, lane, big), -1, keepdims=True)
    le1 = jnp.where(lane == e0, neg, le)
    m1 = jnp.max(le1, -1, keepdims=True)
    e1 = jnp.min(jnp.where(le1 == m1, lane, big), -1, keepdims=True)
    r = jnp.exp(m1 - m0)
    w0 = p_sel / (1.0 + r)
    w1 = p_sel * r / (1.0 + r)

    hot0 = lane == e0
    hot1 = lane == e1
    onehot = jnp.where(hot0 | hot1, 1.0, 0.0)
    ti = lax.broadcasted_iota(I32, (tm, tm), 0)
    tj = lax.broadcasted_iota(I32, (tm, tm), 1)
    before = jnp.where(ti > tj, 1.0, 0.0).astype(BF16)
    base = jnp.dot(before, onehot.astype(BF16), preferred_element_type=F32) + cnt_ref[...]
    r0 = jnp.sum(jnp.where(hot0, base, 0.0), -1, keepdims=True)
    r1 = jnp.sum(jnp.where(hot1, base, 0.0), -1, keepdims=True)
    cnt_ref[...] = cnt_ref[...] + jnp.sum(onehot, 0, keepdims=True)

    info = jnp.zeros((tm, LANES), F32)
    for idx, val in ((INFO_E0, e0.astype(F32)), (INFO_E1, e1.astype(F32)), (INFO_W0, w0),
                     (INFO_W1, w1), (INFO_R0, r0), (INFO_R1, r1)):
        info = jnp.where(lane == idx, val, info)
    info_ref[...] = info


def _router(x2d, norm_w, router_grp, router_exp):
    n, d = x2d.shape
    tm = min(ROW_TILE, n)
    w_r = jnp.zeros((d, LANES), F32).at[:, :N_EXPERTS].set(router_exp)
    w_r = w_r.at[:, GRP_LANE0:GRP_LANE0 + MOE_GROUPS].set(router_grp)
    w_parts = jnp.stack(_split2(w_r))
    return pl.pallas_call(
        _router_body,
        grid=(n // tm,),
        in_specs=[pl.BlockSpec((tm, d), lambda i: (i, 0)), _full((1, d)), _full(w_parts.shape)],
        out_specs=[pl.BlockSpec((tm * ROW_PIECES, LANES), lambda i: (i, 0)),
                   pl.BlockSpec((tm, LANES), lambda i: (i, 0)),
                   _full((1, LANES))],
        out_shape=[jax.ShapeDtypeStruct((n * ROW_PIECES, LANES), F32),
                   jax.ShapeDtypeStruct((n, LANES), F32),
                   jax.ShapeDtypeStruct((1, LANES), F32)],
        compiler_params=_params(("arbitrary",)),
        name="router",
    )(x2d, norm_w.reshape(1, d), w_parts)


ROW_PIECES = D_MODEL // LANES
assert ROW_PIECES == SUBLANES


def _store_token_tiles(ref, rows, value):
    for s in range(ROW_PIECES):
        ref[pl.ds(s, rows, stride=ROW_PIECES), :] = value[:, s * LANES:(s + 1) * LANES]


def _load_token_piece(ref, rows, s):
    return ref[pl.ds(s, rows, stride=ROW_PIECES), :]


SLOT_FREE = -1
INVERT_UNROLL = 8
INVERT_STEP = 1024
DMA_THREADS = 2
TILE_ROWS = MOE_BLOCK * ROW_PIECES


def _start_all(copies):
    for r, cp in enumerate(copies):
        cp.start(priority=r % DMA_THREADS)


def _dest_body(info_ref, start_ref, o_ref):
    info = info_ref[...]
    lane = lax.broadcasted_iota(I32, info.shape, 1)
    lane_f = lane.astype(F32)
    start = start_ref[...]
    out = jnp.zeros(info.shape, F32)
    for slot, (e_lane, r_lane) in enumerate(((INFO_E0, INFO_R0), (INFO_E1, INFO_R1))):
        hit = lane_f == info[:, e_lane:e_lane + 1]
        row = jnp.sum(jnp.where(hit, start, 0.0), -1, keepdims=True) + info[:, r_lane:r_lane + 1]
        out = jnp.where(lane == slot, row, out)
    o_ref[...] = out


def _dest(info, pad_start):
    n = info.shape[0]
    tm = min(ROW_TILE, n)
    start = jnp.zeros((1, LANES), F32).at[0, :N_EXPERTS].set(pad_start.astype(F32))
    rows = pl.pallas_call(
        _dest_body,
        grid=(n // tm,),
        in_specs=[pl.BlockSpec((tm, LANES), lambda i: (i, 0)), _full((1, LANES))],
        out_specs=pl.BlockSpec((tm, LANES), lambda i: (i, 0)),
        out_shape=jax.ShapeDtypeStruct((n, LANES), F32),
        compiler_params=_params(("parallel",)),
        name="dest",
    )(info, start)
    return rows[:, :2].astype(I32).reshape(-1)


def _invert_body(dest_ref, free_hbm, slot_ref, sem):
    step = pl.program_id(0)

    @pl.when(step == 0)
    def _():
        fill = pltpu.make_async_copy(free_hbm, slot_ref, sem)
        fill.start()
        fill.wait()

    def body(i, carry):
        slot_ref[dest_ref[i]] = step * INVERT_STEP + i
        return carry

    lax.fori_loop(0, INVERT_STEP, body, 0, unroll=INVERT_UNROLL)


def _invert(dest, n_rows):
    return pl.pallas_call(
        _invert_body,
        grid=(dest.shape[0] // INVERT_STEP,),
        in_specs=[pl.BlockSpec((INVERT_STEP,), lambda i: (i,), memory_space=pltpu.SMEM),
                  pl.BlockSpec(memory_space=pl.ANY)],
        out_specs=pl.BlockSpec(memory_space=pltpu.SMEM),
        out_shape=jax.ShapeDtypeStruct((n_rows,), I32),
        scratch_shapes=[pltpu.SemaphoreType.DMA(())],
        compiler_params=_params(("arbitrary",)),
        name="invert",
    )(dest, jnp.full((n_rows,), SLOT_FREE, I32))


def _dispatch_body(start_ref, end_ref, used_ref, dest_ref, x_ref, xs_hbm, zbuf, sem, zsem):
    tm = x_ref.shape[0] // ROW_PIECES
    n_blocks = xs_hbm.shape[0] // TILE_ROWS

    @pl.when(pl.program_id(0) == 0)
    def _():
        zbuf[...] = jnp.zeros(zbuf.shape, F32)

        def fill(block):
            return pltpu.make_async_copy(
                zbuf, xs_hbm.at[pl.ds(pl.multiple_of(block * TILE_ROWS, TILE_ROWS), TILE_ROWS)], zsem)

        def each_fill(action):
            def expert(e, carry):
                @pl.when(end_ref[e] > start_ref[e])
                def _():
                    action(fill(end_ref[e] // MOE_BLOCK - 1))
                return carry

            def tail(blk, carry):
                @pl.when(blk >= used_ref[0])
                def _():
                    action(fill(blk))
                return carry

            lax.fori_loop(0, N_EXPERTS, expert, 0)
            lax.fori_loop(0, n_blocks, tail, 0)

        each_fill(lambda cp: cp.start())
        each_fill(lambda cp: cp.wait())

    copies = [pltpu.make_async_copy(
        x_ref.at[pl.ds(t * ROW_PIECES, ROW_PIECES)],
        xs_hbm.at[pl.ds(pl.multiple_of(dest_ref[2 * t + slot], ROW_PIECES), ROW_PIECES)], sem)
        for t in range(tm) for slot in range(2)]
    _start_all(copies)
    for cp in copies:
        cp.wait()


def _dispatch(xn_tiles, dest_rows, pad_start, pad_end, n_used, n_blocks):
    n = xn_tiles.shape[0] // ROW_PIECES
    tm = min(ROW_TILE, n)
    grid_spec = pltpu.PrefetchScalarGridSpec(
        num_scalar_prefetch=3,
        grid=(n // tm,),
        in_specs=[pl.BlockSpec((2 * tm,), lambda i, *_: (i,), memory_space=pltpu.SMEM),
                  pl.BlockSpec((tm * ROW_PIECES, LANES), lambda i, *_: (i, 0))],
        out_specs=pl.BlockSpec(memory_space=pl.ANY),
        scratch_shapes=[pltpu.VMEM((TILE_ROWS, LANES), F32),
                        pltpu.SemaphoreType.DMA(()), pltpu.SemaphoreType.DMA(())],
    )
    return pl.pallas_call(
        _dispatch_body,
        grid_spec=grid_spec,
        out_shape=jax.ShapeDtypeStruct((n_blocks * TILE_ROWS, LANES), F32),
        compiler_params=_params(("arbitrary",)),
        name="dispatch",
    )(pad_start, pad_end, n_used, dest_rows, xn_tiles)


def _scatter_copies(idx_ref, buf, hbm, sem):
    return [pltpu.make_async_copy(
        buf.at[pl.ds(r * ROW_PIECES, ROW_PIECES)],
        hbm.at[pl.ds(pl.multiple_of(idx_ref[0, r], ROW_PIECES), ROW_PIECES)], sem)
        for r in range(MOE_BLOCK)]


def _ffn_body(blk_ref, last_ref, sprev_ref, scur_ref, xs_ref, wg_ref, wu_ref, wd_ref, yt_hbm,
              ybuf, wg16, wu16, wd16, ssem):
    b = pl.program_id(0)
    last = last_ref[0]

    @pl.when(b <= last)
    def _():
        slot = b % 2
        other = 1 - slot

        @pl.when(b == 0)
        def _():
            ybuf[...] = jnp.zeros(ybuf.shape, F32)

        @pl.when(b > 0)
        def _():
            for cp in _scatter_copies(sprev_ref, ybuf.at[slot], yt_hbm, ssem.at[slot]):
                cp.wait()

        @pl.when((b == 0) | (blk_ref[b] != blk_ref[jnp.maximum(b - 1, 0)]))
        def _():
            wg16[...] = wg_ref[...].astype(BF16)
            wu16[...] = wu_ref[...].astype(BF16)
            wd16[...] = wd_ref[...].astype(BF16)

        _start_all(_scatter_copies(scur_ref, ybuf.at[other], yt_hbm, ssem.at[other]))
        xb = jnp.concatenate([_load_token_piece(xs_ref, MOE_BLOCK, s).astype(BF16)
                              for s in range(ROW_PIECES)], axis=1)
        g = jnp.dot(xb, wg16[...], preferred_element_type=F32)
        u = jnp.dot(xb, wu16[...], preferred_element_type=F32)
        hb = (_silu(g) * u).astype(BF16)
        _store_token_tiles(ybuf.at[slot], MOE_BLOCK, jnp.dot(hb, wd16[...], preferred_element_type=F32))

        @pl.when(b == last)
        def _():
            for cp in _scatter_copies(scur_ref, ybuf.at[other], yt_hbm, ssem.at[other]):
                cp.wait()


def _ffn(xs, scatter_rows, blk_expert, last_step, n_out_rows, layer, w_gate, w_up, w_down):
    d = D_MODEL
    de = w_gate.shape[-1]
    n_steps = blk_expert.shape[0]
    n_blocks = xs.shape[0] // TILE_ROWS
    idx_spec = lambda shift: pl.BlockSpec(
        (None, 1, MOE_BLOCK), lambda b, e, u: (jnp.maximum(b + shift, 0), 0, 0),
        memory_space=pltpu.SMEM)
    w_spec = lambda rows, cols: pl.BlockSpec((None, None, rows, cols),
                                             lambda b, e, u: (layer, e[b], 0, 0))
    grid_spec = pltpu.PrefetchScalarGridSpec(
        num_scalar_prefetch=2,
        grid=(n_steps,),
        in_specs=[idx_spec(-1), idx_spec(0),
                  pl.BlockSpec((TILE_ROWS, LANES), lambda b, e, u: (jnp.minimum(b, n_blocks - 1), 0)),
                  w_spec(d, de), w_spec(d, de), w_spec(de, d)],
        out_specs=pl.BlockSpec(memory_space=pl.ANY),
        scratch_shapes=[pltpu.VMEM((2, TILE_ROWS, LANES), F32),
                        pltpu.VMEM((d, de), BF16), pltpu.VMEM((d, de), BF16),
                        pltpu.VMEM((de, d), BF16), pltpu.SemaphoreType.DMA((2,))],
    )
    return pl.pallas_call(
        _ffn_body,
        grid_spec=grid_spec,
        out_shape=jax.ShapeDtypeStruct((n_out_rows * ROW_PIECES, LANES), F32),
        compiler_params=_params(("arbitrary",)),
        name="expert_ffn",
    )(blk_expert, last_step, scatter_rows, scatter_rows, xs, w_gate, w_up, w_down)


def _combine_body(x_ref, y0_ref, y1_ref, info_ref, nw_ref, *out_refs):
    tm = x_ref.shape[0]
    info = info_ref[...]
    w0 = info[:, INFO_W0:INFO_W0 + 1]
    w1 = info[:, INFO_W1:INFO_W1 + 1]
    x = jnp.concatenate(
        [x_ref[:, s * LANES:(s + 1) * LANES] + w0 * _load_token_piece(y0_ref, tm, s)
         + w1 * _load_token_piece(y1_ref, tm, s) for s in range(ROW_PIECES)], axis=1)
    xn = x * lax.rsqrt(jnp.mean(x * x, -1, keepdims=True) + EPS) * nw_ref[...]
    if len(out_refs) == 2:
        out_refs[0][...] = x
    out_refs[-1][...] = xn.astype(out_refs[-1].dtype)


def _combine(x2d, yt, info, next_norm_w, final):
    n, d = x2d.shape
    tm = MOE_BLOCK
    row = lambda width: pl.BlockSpec((tm, width), lambda i: (i, 0))
    slot0 = pl.BlockSpec((tm * ROW_PIECES, LANES), lambda i: (i, 0))
    slot1 = pl.BlockSpec((tm * ROW_PIECES, LANES), lambda i: (i + n // tm + 1, 0))
    out_shape = [jax.ShapeDtypeStruct((n, d), F32)]
    if not final:
        out_shape.append(jax.ShapeDtypeStruct((n, d), BF16))
    return pl.pallas_call(
        _combine_body,
        grid=(n // tm,),
        in_specs=[row(d), slot0, slot1, row(LANES), _full((1, d))],
        out_specs=[row(d)] * len(out_shape),
        out_shape=out_shape,
        compiler_params=_params(("parallel",)),
        name="combine",
    )(x2d, yt, yt, info, next_norm_w.reshape(1, d))


def _hier_moe(x2d, norm_w, router_grp, router_exp, layer, w_gate, w_up, w_down, next_norm_w, final):
    n, _ = x2d.shape
    xn_tiles, info, cnt = _router(x2d, norm_w, router_grp, router_exp)
    counts = cnt[0, :N_EXPERTS].astype(I32)
    padded = (counts + MOE_BLOCK - 1) // MOE_BLOCK * MOE_BLOCK
    pad_end = jnp.cumsum(padded)
    pad_start = pad_end - padded
    n_blocks = (2 * n + N_EXPERTS * (MOE_BLOCK - 1)) // MOE_BLOCK + 1
    n_used = (pad_end[-1:] // MOE_BLOCK).astype(I32)
    dest = _dest(info, pad_start)
    xs = _dispatch(xn_tiles, dest * ROW_PIECES, pad_start, pad_end, n_used, n_blocks)
    slots = _invert(dest, n_blocks * MOE_BLOCK)
    spare = n + jnp.arange(MOE_BLOCK, dtype=I32)
    scatter_rows = jnp.where(slots < 0, jnp.tile(spare, n_blocks),
                             (slots & 1) * (n + MOE_BLOCK) + (slots >> 1))
    scatter_rows = jnp.concatenate([spare, scatter_rows, spare]) * ROW_PIECES
    blk_start = jnp.arange(n_blocks + 1, dtype=I32) * MOE_BLOCK
    blk_expert = jnp.minimum(jnp.sum((pad_end[None, :] <= blk_start[:, None]).astype(I32), axis=1),
                             N_EXPERTS - 1)
    yt = _ffn(xs, scatter_rows.reshape(n_blocks + 2, 1, MOE_BLOCK), blk_expert, n_used,
              2 * n + MOE_BLOCK, layer, w_gate, w_up, w_down)
    return _combine(x2d, yt, info, next_norm_w, final)


def _mixer_block(x2d, xn2d, bsz, w_in, gdn_conv_w, gdn_a_log, gdn_dt_bias, gdn_norm_w,
                 conf_conv_w, conf_conv_b, conf_ln_w, conf_ln_b, gmlp_ln_w, gmlp_ln_b,
                 gmlp_w_s, gmlp_b_s, sc_conv_w, w_branch, w_out):
    n, d = x2d.shape
    t_len = n // bsz
    xn = xn2d.reshape(bsz, t_len, d)
    c0 = 2 * GDN_QK + 2 * GDN_V
    c1 = c0 + GDN_HEADS
    c2 = c1 + GDN_HEADS
    c3 = c2 + 2 * BRANCH_WIDTH
    c4 = c3 + 2 * BRANCH_WIDTH
    c5 = c4 + 3 * BRANCH_WIDTH
    w16 = w_in.astype(BF16)
    ys = (
        _gdn(xn, w16[:, :c0], w16[:, c0:c1], w16[:, c1:c2], gdn_conv_w, gdn_a_log, gdn_dt_bias,
             gdn_norm_w),
        _conformer(xn, w16[:, c2:c3], conf_conv_w, conf_conv_b, conf_ln_w, conf_ln_b),
        _gmlp(xn, w16[:, c3:c4], gmlp_ln_w, gmlp_ln_b, gmlp_w_s, gmlp_b_s),
        _shortconv(xn, w16[:, c4:c5], sc_conv_w),
    )
    ys = [y.reshape(bsz * t_len, BRANCH_WIDTH) for y in ys]
    w_gate = w16[:, c5:].reshape(d, N_BRANCH, d).transpose(1, 0, 2)
    return _merge(x2d, xn2d, ys, w_gate, w_branch.astype(BF16), w_out.astype(BF16))


def kernel(x, mix_norm_w, w_in, gdn_conv_w, gdn_A_log, gdn_dt_bias, gdn_norm_w, conf_conv_w,
           conf_conv_b, conf_ln_w, conf_ln_b, gmlp_ln_w, gmlp_ln_b, gmlp_w_s, gmlp_b_s, sc_conv_w,
           w_branch, w_out, moe_norm_w, router_grp, router_exp, w_gate, w_up, w_down, final_norm_w):
    bsz, t_len, d = x.shape
    depth = mix_norm_w.shape[0]
    x2d = x.reshape(bsz * t_len, d)
    xn2d = _rmsnorm(x2d, mix_norm_w[0], BF16)
    for l in range(depth):
        x2d = _mixer_block(x2d, xn2d, bsz, w_in[l], gdn_conv_w[l], gdn_A_log[l], gdn_dt_bias[l],
                           gdn_norm_w[l], conf_conv_w[l], conf_conv_b[l], conf_ln_w[l],
                           conf_ln_b[l], gmlp_ln_w[l], gmlp_ln_b[l], gmlp_w_s[l], gmlp_b_s[l],
                           sc_conv_w[l], w_branch[l], w_out[l])
        final = l == depth - 1
        next_norm_w = final_norm_w if final else mix_norm_w[l + 1]
        outs = _hier_moe(x2d, moe_norm_w[l], router_grp[l], router_exp[l], l, w_gate, w_up, w_down,
                         next_norm_w, final)
        if final:
            return outs[0].reshape(bsz, t_len, d)
        x2d, xn2d = outs
```

```python
import functools

import jax
import jax.numpy as jnp
from jax import lax
from jax.experimental import pallas as pl
from jax.experimental.pallas import tpu as pltpu

F32 = jnp.float32
BF16 = jnp.bfloat16
I32 = jnp.int32

D_MODEL = 1024
BRANCH_WIDTH = 512
N_BRANCH = 4
GDN_HEADS = 4
GDN_DK = 128
GDN_DV = 128
GDN_CONV = 4
GDN_CHUNK = 64
GDN_QK = GDN_HEADS * GDN_DK
GDN_V = GDN_HEADS * GDN_DV
CONF_KERNEL = 31
GMLP_GROUPS = 4
GMLP_CHUNK = 128
SC_KERNEL = 3
MOE_GROUPS = 8
MOE_PER_GROUP = 8
N_EXPERTS = MOE_GROUPS * MOE_PER_GROUP
D_EXPERT = 512
MOE_BLOCK = 256
EPS = 1e-6

LANES = 128
SUBLANES = 8
VMEM_LIMIT = 56 * 1024 * 1024

SEQ_TILE = 512
ROW_TILE = 512
CONF_HALO = 32
SMALL_HALO = SUBLANES
CONV_ROWS = 64

INFO_E0, INFO_E1, INFO_W0, INFO_W1, INFO_R0, INFO_R1 = range(6)
GRP_LANE0 = N_EXPERTS


def _params(sem):
    return pltpu.CompilerParams(dimension_semantics=sem, vmem_limit_bytes=VMEM_LIMIT)


def _full(shape):
    nd = len(shape)
    return pl.BlockSpec(shape, lambda *_: (0,) * nd)


def _bdot(a, b):
    return jnp.dot(a.astype(BF16), b.astype(BF16), preferred_element_type=F32)


def _sigmoid(x):
    return 1.0 / (1.0 + jnp.exp(-x))


def _silu(x):
    return x * _sigmoid(x)


def _rms_body(x_ref, w_ref, o_ref):
    x = x_ref[...]
    ms = jnp.mean(x * x, -1, keepdims=True)
    o_ref[...] = (x * lax.rsqrt(ms + EPS) * w_ref[...]).astype(o_ref.dtype)


def _rmsnorm(x2d, w, out_dtype):
    n, d = x2d.shape
    tm = min(ROW_TILE, n)
    return pl.pallas_call(
        _rms_body,
        grid=(n // tm,),
        in_specs=[pl.BlockSpec((tm, d), lambda i: (i, 0)), _full((1, d))],
        out_specs=pl.BlockSpec((tm, d), lambda i: (i, 0)),
        out_shape=jax.ShapeDtypeStruct((n, d), out_dtype),
        compiler_params=_params(("parallel",)),
        name="rmsnorm",
    )(x2d, w.reshape(1, d))


def _carry_halo(buf, halo, tile):
    t = pl.program_id(1)

    @pl.when(t == 0)
    def _():
        buf[0:halo, :] = jnp.zeros((halo, buf.shape[1]), buf.dtype)

    @pl.when(t != 0)
    def _():
        buf[0:halo, :] = buf[tile:tile + halo, :]


def _causal_taps(buf, cw_ref, halo, row0, rows, init):
    k_w = cw_ref.shape[0]
    acc = init
    for k in range(k_w):
        off = halo - (k_w - 1) + k + row0
        acc = acc + cw_ref[k:k + 1, :] * buf[off:off + rows, :]
    return acc


def _causal_taps_by_phase(buf, cw_ref, halo, row0, rows, init):
    k_w = cw_ref.shape[0]
    base = halo - (k_w - 1)
    acc = init
    for phase in range(SUBLANES):
        taps = [k for k in range(k_w) if (base + k) % SUBLANES == phase]
        span = rows + (SUBLANES if phase else 0)
        part = None
        for k in taps:
            off = base + k - phase + row0
            term = cw_ref[k:k + 1, :] * buf[off:off + span, :]
            part = term if part is None else part + term
        if part is not None:
            acc = acc + part[phase:phase + rows, :]
    return acc


def _layernorm(x, w, b):
    mu = jnp.mean(x, -1, keepdims=True)
    xc = x - mu
    var = jnp.mean(xc * xc, -1, keepdims=True)
    return xc * lax.rsqrt(var + EPS) * w + b


def _seq_call(body, xn, weights, out_width, scratch, name):
    bsz, t_len, d = xn.shape
    tile = min(SEQ_TILE, t_len)
    in_specs = [pl.BlockSpec((None, tile, d), lambda b, t: (b, t, 0))]
    in_specs += [_full(w.shape) for w in weights]
    return pl.pallas_call(
        body,
        grid=(bsz, t_len // tile),
        in_specs=in_specs,
        out_specs=pl.BlockSpec((None, tile, out_width), lambda b, t: (b, t, 0)),
        out_shape=jax.ShapeDtypeStruct((bsz, t_len, out_width), BF16),
        scratch_shapes=scratch(tile),
        compiler_params=_params(("parallel", "arbitrary")),
        name=name,
    )(xn, *weights)


def _conf_body(xn_ref, w_ref, cw_ref, cb_ref, lw_ref, lb_ref, o_ref, ybuf):
    tile, width = o_ref.shape
    _carry_halo(ybuf, CONF_HALO, tile)
    h = jnp.dot(xn_ref[...], w_ref[...], preferred_element_type=F32)
    ybuf[CONF_HALO:CONF_HALO + tile, :] = h[:, :width] * _sigmoid(h[:, width:])
    for r in range(tile // CONV_ROWS):
        init = jnp.broadcast_to(cb_ref[...], (CONV_ROWS, width))
        y = _causal_taps_by_phase(ybuf, cw_ref, CONF_HALO, r * CONV_ROWS, CONV_ROWS, init)
        y = _layernorm(y, lw_ref[...], lb_ref[...])
        o_ref[r * CONV_ROWS:(r + 1) * CONV_ROWS, :] = _silu(y).astype(o_ref.dtype)


def _conformer(xn, w, conv_w, conv_b, ln_w, ln_b):
    width = BRANCH_WIDTH
    weights = [w, conv_w, conv_b.reshape(1, width), ln_w.reshape(1, width), ln_b.reshape(1, width)]
    scratch = lambda tile: [pltpu.VMEM((CONF_HALO + tile, width), F32)]
    return _seq_call(_conf_body, xn, weights, width, scratch, "conformer")


def _sc_body(xn_ref, w_ref, cw_ref, o_ref, ubuf, bbuf):
    tile, width = o_ref.shape
    _carry_halo(ubuf, SMALL_HALO, tile)
    h = jnp.dot(xn_ref[...], w_ref[...], preferred_element_type=F32)
    bbuf[...] = h[:, :width]
    ubuf[SMALL_HALO:SMALL_HALO + tile, :] = h[:, width:2 * width] * h[:, 2 * width:]
    for r in range(tile // CONV_ROWS):
        rows = slice(r * CONV_ROWS, (r + 1) * CONV_ROWS)
        init = jnp.zeros((CONV_ROWS, width), F32)
        y = _causal_taps(ubuf, cw_ref, SMALL_HALO, r * CONV_ROWS, CONV_ROWS, init)
        o_ref[rows, :] = (bbuf[rows, :] * y).astype(o_ref.dtype)


def _shortconv(xn, w, conv_w):
    width = BRANCH_WIDTH
    scratch = lambda tile: [pltpu.VMEM((SMALL_HALO + tile, width), F32), pltpu.VMEM((tile, width), F32)]
    return _seq_call(_sc_body, xn, [w, conv_w], width, scratch, "shortconv")


def _gmlp_body(xn_ref, w_ref, lw_ref, lb_ref, ws_ref, bs_ref, o_ref, ubuf, vbuf):
    tile, width = o_ref.shape
    gc = width // GMLP_GROUPS
    h = jnp.dot(xn_ref[...], w_ref[...], preferred_element_type=F32)
    g = jax.nn.gelu(h)
    ubuf[...] = g[:, :width]
    vbuf[...] = _layernorm(g[:, width:], lw_ref[...], lb_ref[...]).astype(BF16)
    row = lax.broadcasted_iota(I32, (GMLP_CHUNK, GMLP_CHUNK), 0)
    col = lax.broadcasted_iota(I32, (GMLP_CHUNK, GMLP_CHUNK), 1)
    for gi in range(GMLP_GROUPS):
        w_c = jnp.where(row >= col, ws_ref[gi], 0.0).astype(BF16)
        cols = slice(gi * gc, (gi + 1) * gc)
        for c in range(tile // GMLP_CHUNK):
            rows = slice(c * GMLP_CHUNK, (c + 1) * GMLP_CHUNK)
            mixed = jnp.dot(w_c, vbuf[rows, cols], preferred_element_type=F32) + bs_ref[gi]
            o_ref[rows, cols] = (ubuf[rows, cols] * mixed).astype(o_ref.dtype)


def _gmlp(xn, w, ln_w, ln_b, w_s, b_s):
    width = BRANCH_WIDTH
    gc = width // GMLP_GROUPS
    b_full = jnp.broadcast_to(b_s[:, :, None], (GMLP_GROUPS, GMLP_CHUNK, gc))
    weights = [w, ln_w.reshape(1, width), ln_b.reshape(1, width), w_s, b_full]
    scratch = lambda tile: [pltpu.VMEM((tile, width), F32), pltpu.VMEM((tile, width), BF16)]
    return _seq_call(_gmlp_body, xn, weights, width, scratch, "gmlp")


def _softplus(x):
    return jnp.maximum(x, 0.0) + jnp.log(1.0 + jnp.exp(-jnp.abs(x)))


def _bmm(a, b):
    return jnp.einsum("nij,njk->nik", a.astype(BF16), b.astype(BF16), preferred_element_type=F32)


def _bmm_nt(a, b):
    return jnp.einsum("nid,njd->nij", a.astype(BF16), b.astype(BF16), preferred_element_type=F32)


def _bmm_tn(a, b):
    return jnp.einsum("ncd,nce->nde", a.astype(BF16), b.astype(BF16), preferred_element_type=F32)


def _unit_lower_inverse(strict_lower, eye):
    size = strict_lower.shape[-1]
    inv = eye - strict_lower
    power = strict_lower
    span = 2
    while span < size:
        power = _bmm(power, power)
        inv = inv + _bmm(inv, power)
        span *= 2
    return inv


def _split3(x):
    hi = x.astype(BF16)
    r1 = x - hi.astype(F32)
    mid = r1.astype(BF16)
    lo = (r1 - mid.astype(F32)).astype(BF16)
    return hi, mid, lo


def _gdn_body(xn_ref, w_ref, wb_ref, wa_ref, cw_ref, alog_ref, dtb_ref, nw_ref, o_ref,
              hbuf, qbuf, kbuf, vbuf, zbuf, gbuf, bbuf, s_ref, wc_ref, bc_ref, qp_ref, op_ref, egl_ref):
    tile = o_ref.shape[0]
    chunk = GDN_CHUNK
    t = pl.program_id(1)
    _carry_halo(hbuf, SMALL_HALO, tile)

    @pl.when(t == 0)
    def _():
        s_ref[...] = jnp.zeros(s_ref.shape, F32)

    xn = xn_ref[...]
    h = jnp.dot(xn, w_ref[...], preferred_element_type=F32)
    n_conv = 2 * GDN_QK + GDN_V
    hbuf[SMALL_HALO:SMALL_HALO + tile, :] = h[:, :n_conv]
    zbuf[...] = h[:, n_conv:]
    bbuf[...] = _sigmoid(jnp.dot(xn, wb_ref[...], preferred_element_type=F32))
    a_raw = jnp.dot(xn, wa_ref[...], preferred_element_type=F32)
    gbuf[...] = -jnp.exp(alog_ref[...]) * _softplus(a_raw + dtb_ref[...])

    for r in range(tile // CONV_ROWS):
        rows = slice(r * CONV_ROWS, (r + 1) * CONV_ROWS)
        init = jnp.zeros((CONV_ROWS, n_conv), F32)
        qkv = _silu(_causal_taps(hbuf, cw_ref, SMALL_HALO, r * CONV_ROWS, CONV_ROWS, init))
        for hd in range(GDN_HEADS):
            q = qkv[:, hd * GDN_DK:(hd + 1) * GDN_DK]
            k = qkv[:, GDN_QK + hd * GDN_DK:GDN_QK + (hd + 1) * GDN_DK]
            q = q * lax.rsqrt(jnp.sum(q * q, -1, keepdims=True) + EPS) * (GDN_DK ** -0.5)
            k = k * lax.rsqrt(jnp.sum(k * k, -1, keepdims=True) + EPS)
            qbuf[rows, hd * GDN_DK:(hd + 1) * GDN_DK] = q
            kbuf[rows, hd * GDN_DK:(hd + 1) * GDN_DK] = k
        vbuf[rows, :] = qkv[:, 2 * GDN_QK:]

    n_chunks = tile // chunk
    n_batch = GDN_HEADS * n_chunks
    row = lax.broadcasted_iota(I32, (chunk, chunk), 0)
    col = lax.broadcasted_iota(I32, (chunk, chunk), 1)
    tril = row >= col
    strict = row > col
    eye = jnp.where(row == col, 1.0, 0.0)

    def per_head(buf, width):
        return jnp.concatenate(
            [buf[:, hd * width:(hd + 1) * width].reshape(n_chunks, chunk, width)
             for hd in range(GDN_HEADS)], axis=0)

    g_lo, g_mid, g_hi = reversed(_split3(gbuf[...].reshape(n_chunks, chunk, LANES)))
    ones_tril = jnp.broadcast_to(jnp.where(tril, 1.0, 0.0).astype(BF16), (n_chunks, chunk, chunk))
    g_cum = _bmm(ones_tril, g_lo) + _bmm(ones_tril, g_mid) + _bmm(ones_tril, g_hi)
    g_cum_t = [g_cum[c].T for c in range(n_chunks)]
    g_col = jnp.concatenate([g_cum[:, :, hd:hd + 1] for hd in range(GDN_HEADS)], axis=0)
    g_row = jnp.concatenate(
        [g_cum_t[c][hd:hd + 1, :][None] for hd in range(GDN_HEADS) for c in range(n_chunks)], axis=0)
    b3 = bbuf[...].reshape(n_chunks, chunk, LANES)
    b_col = jnp.concatenate([b3[:, :, hd:hd + 1] for hd in range(GDN_HEADS)], axis=0)
    g_last = g_col[:, chunk - 1:chunk, :]

    q = per_head(qbuf[...], GDN_DK)
    k = per_head(kbuf[...], GDN_DK)
    v = per_head(vbuf[...], GDN_DV)
    decay = jnp.where(tril, jnp.exp(jnp.where(tril, g_col - g_row, 0.0)), 0.0)
    e_g = jnp.exp(g_col)
    k_beta = k * b_col
    inv = _unit_lower_inverse(jnp.where(strict, _bmm_nt(k_beta, k) * decay, 0.0), eye)
    uw = _bmm(inv, jnp.concatenate([v * b_col, k_beta * e_g], axis=2))
    u = uw[:, :, :GDN_DV]
    w = uw[:, :, GDN_DV:]
    attn = _bmm_nt(q, k) * decay
    k_dec = k * jnp.exp(g_last - g_col)
    wc_ref[...] = _bmm_tn(k_dec, w).astype(BF16)
    bc_ref[...] = _bmm_tn(k_dec, u)
    qp_ref[...] = (q * e_g - _bmm(attn, w)).astype(BF16)
    op_ref[...] = _bmm(attn, u)
    egl_ref[...] = jnp.broadcast_to(jnp.exp(g_last), (n_batch, 1, GDN_DV))

    for c in range(n_chunks):
        rows = slice(c * chunk, (c + 1) * chunk)
        for hd in range(GDN_HEADS):
            n = hd * n_chunks + c
            cols = slice(hd * GDN_DV, (hd + 1) * GDN_DV)
            state = s_ref[hd]
            s16 = state.astype(BF16)
            o = jnp.dot(qp_ref[n], s16, preferred_element_type=F32) + op_ref[n]
            s_ref[hd] = (state * egl_ref[n] + bc_ref[n]
                         - jnp.dot(wc_ref[n], s16, preferred_element_type=F32))
            o = o * lax.rsqrt(jnp.mean(o * o, -1, keepdims=True) + EPS) * nw_ref[...]
            o_ref[rows, cols] = (o * _silu(zbuf[rows, cols])).astype(o_ref.dtype)


def _lane_pad(vec):
    return jnp.zeros((1, LANES), F32).at[0, :vec.shape[0]].set(vec)


def _gdn(xn, w, w_beta, w_a, conv_w, a_log, dt_bias, norm_w):
    pad = ((0, 0), (0, LANES - GDN_HEADS))
    weights = [w, jnp.pad(w_beta, pad), jnp.pad(w_a, pad), conv_w,
               _lane_pad(a_log), _lane_pad(dt_bias), norm_w.reshape(1, GDN_DV)]
    n_conv = 2 * GDN_QK + GDN_V

    def scratch(tile):
        n_batch = GDN_HEADS * (tile // GDN_CHUNK)
        return [pltpu.VMEM((SMALL_HALO + tile, n_conv), F32),
                pltpu.VMEM((tile, GDN_QK), F32), pltpu.VMEM((tile, GDN_QK), F32),
                pltpu.VMEM((tile, GDN_V), F32), pltpu.VMEM((tile, GDN_V), F32),
                pltpu.VMEM((tile, LANES), F32), pltpu.VMEM((tile, LANES), F32),
                pltpu.VMEM((GDN_HEADS, GDN_DK, GDN_DV), F32),
                pltpu.VMEM((n_batch, GDN_DK, GDN_DK), BF16),
                pltpu.VMEM((n_batch, GDN_DK, GDN_DV), F32),
                pltpu.VMEM((n_batch, GDN_CHUNK, GDN_DK), BF16),
                pltpu.VMEM((n_batch, GDN_CHUNK, GDN_DV), F32),
                pltpu.VMEM((n_batch, 1, GDN_DV), F32)]

    return _seq_call(_gdn_body, xn, weights, GDN_V, scratch, "gdn")


def _merge_body(x_ref, xn_ref, y0_ref, y1_ref, y2_ref, y3_ref, wg_ref, wb_ref, wo_ref, o_ref):
    xn = xn_ref[...]
    d = xn.shape[1]
    merged = None
    for i, y_ref in enumerate((y0_ref, y1_ref, y2_ref, y3_ref)):
        gate = _sigmoid(jnp.dot(xn, wg_ref[:, i * d:(i + 1) * d], preferred_element_type=F32))
        term = gate * jnp.dot(y_ref[...], wb_ref[i], preferred_element_type=F32)
        merged = term if merged is None else merged + term
    o_ref[...] = x_ref[...] + jnp.dot(merged.astype(BF16), wo_ref[...], preferred_element_type=F32)


def _merge(x2d, xn2d, ys, w_gate, w_branch, w_out):
    n, d = x2d.shape
    tm = min(ROW_TILE, n)
    row = lambda width: pl.BlockSpec((tm, width), lambda i: (i, 0))
    return pl.pallas_call(
        _merge_body,
        grid=(n // tm,),
        in_specs=[row(d), row(d)] + [row(BRANCH_WIDTH)] * N_BRANCH
        + [_full(w_gate.shape), _full(w_branch.shape), _full(w_out.shape)],
        out_specs=row(d),
        out_shape=jax.ShapeDtypeStruct((n, d), F32),
        compiler_params=_params(("parallel",)),
        name="merge",
    )(x2d, xn2d, *ys, w_gate, w_branch, w_out)


def _split2(x):
    hi = x.astype(BF16)
    lo = (x - hi.astype(F32)).astype(BF16)
    return hi, lo


def _router_body(x_ref, nw_ref, wr_ref, xn_ref, info_ref, cnt_ref):
    tm = x_ref.shape[0]

    @pl.when(pl.program_id(0) == 0)
    def _():
        cnt_ref[...] = jnp.zeros(cnt_ref.shape, F32)

    x = x_ref[...]
    xn = x * lax.rsqrt(jnp.mean(x * x, -1, keepdims=True) + EPS) * nw_ref[...]
    _store_token_tiles(xn_ref, tm, xn)
    x_hi, x_lo = _split2(xn)
    logits = (jnp.dot(x_lo, wr_ref[0], preferred_element_type=F32)
              + jnp.dot(x_hi, wr_ref[1], preferred_element_type=F32)
              + jnp.dot(x_hi, wr_ref[0], preferred_element_type=F32))

    lane = lax.broadcasted_iota(I32, (tm, LANES), 1)
    neg = jnp.float32(-jnp.inf)
    big = jnp.int32(2 * LANES)
    is_grp = (lane >= GRP_LANE0) & (lane < GRP_LANE0 + MOE_GROUPS)
    lg = jnp.where(is_grp, logits, neg)
    mg = jnp.max(lg, -1, keepdims=True)
    grp = jnp.min(jnp.where(lg == mg, lane, big), -1, keepdims=True) - GRP_LANE0
    p_sel = 1.0 / jnp.sum(jnp.where(is_grp, jnp.exp(lg - mg), 0.0), -1, keepdims=True)

    in_grp = (lane < N_EXPERTS) & ((lane // MOE_PER_GROUP) == grp)
    le = jnp.where(in_grp, logits, neg)
    m0 = jnp.max(le, -1, keepdims=True)
    e0 = jnp.min(jnp.where(le == m0, lane, big), -1, keepdims=True)
    le1 = jnp.where(lane == e0, neg, le)
    m1 = jnp.max(le1, -1, keepdims=True)
    e1 = jnp.min(jnp.where(le1 == m1, lane, big), -1, keepdims=True)
    r = jnp.exp(m1 - m0)
    w0 = p_sel / (1.0 + r)
    w1 = p_sel * r / (1.0 + r)

    hot0 = lane == e0
    hot1 = lane == e1
    onehot = jnp.where(hot0 | hot1, 1.0, 0.0)
    ti = lax.broadcasted_iota(I32, (tm, tm), 0)
    tj = lax.broadcasted_iota(I32, (tm, tm), 1)
    before = jnp.where(ti > tj, 1.0, 0.0).astype(BF16)
    base = jnp.dot(before, onehot.astype(BF16), preferred_element_type=F32) + cnt_ref[...]
    r0 = jnp.sum(jnp.where(hot0, base, 0.0), -1, keepdims=True)
    r1 = jnp.sum(jnp.where(hot1, base, 0.0), -1, keepdims=True)
    cnt_ref[...] = cnt_ref[...] + jnp.sum(onehot, 0, keepdims=True)

    info = jnp.zeros((tm, LANES), F32)
    for idx, val in ((INFO_E0, e0.astype(F32)), (INFO_E1, e1.astype(F32)), (INFO_W0, w0),
                     (INFO_W1, w1), (INFO_R0, r0), (INFO_R1, r1)):
        info = jnp.where(lane == idx, val, info)
    info_ref[...] = info


def _router(x2d, norm_w, router_grp, router_exp):
    n, d = x2d.shape
    tm = min(ROW_TILE, n)
    w_r = jnp.zeros((d, LANES), F32).at[:, :N_EXPERTS].set(router_exp)
    w_r = w_r.at[:, GRP_LANE0:GRP_LANE0 + MOE_GROUPS].set(router_grp)
    w_parts = jnp.stack(_split2(w_r))
    return pl.pallas_call(
        _router_body,
        grid=(n // tm,),
        in_specs=[pl.BlockSpec((tm, d), lambda i: (i, 0)), _full((1, d)), _full(w_parts.shape)],
        out_specs=[pl.BlockSpec((tm * ROW_PIECES, LANES), lambda i: (i, 0)),
                   pl.BlockSpec((tm, LANES), lambda i: (i, 0)),
                   _full((1, LANES))],
        out_shape=[jax.ShapeDtypeStruct((n * ROW_PIECES, LANES), F32),
                   jax.ShapeDtypeStruct((n, LANES), F32),
                   jax.ShapeDtypeStruct((1, LANES), F32)],
        compiler_params=_params(("arbitrary",)),
        name="router",
    )(x2d, norm_w.reshape(1, d), w_parts)


ROW_PIECES = D_MODEL // LANES
assert ROW_PIECES == SUBLANES


def _store_token_tiles(ref, rows, value):
    for s in range(ROW_PIECES):
        ref[pl.ds(s, rows, stride=ROW_PIECES), :] = value[:, s * LANES:(s + 1) * LANES]


def _load_token_piece(ref, rows, s):
    return ref[pl.ds(s, rows, stride=ROW_PIECES), :]


SLOT_FREE = -1
DMA_THREADS = 2
TILE_ROWS = MOE_BLOCK * ROW_PIECES


def _start_all(copies):
    for r, cp in enumerate(copies):
        cp.start(priority=r % DMA_THREADS)


def _dest_body(info_ref, start_ref, o_ref):
    info = info_ref[...]
    lane = lax.broadcasted_iota(I32, info.shape, 1)
    lane_f = lane.astype(F32)
    start = start_ref[...]
    out = jnp.zeros(info.shape, F32)
    for slot, (e_lane, r_lane) in enumerate(((INFO_E0, INFO_R0), (INFO_E1, INFO_R1))):
        hit = lane_f == info[:, e_lane:e_lane + 1]
        row = jnp.sum(jnp.where(hit, start, 0.0), -1, keepdims=True) + info[:, r_lane:r_lane + 1]
        out = jnp.where(lane == slot, row, out)
    o_ref[...] = out


def _dest(info, pad_start):
    n = info.shape[0]
    tm = min(4 * ROW_TILE, n)
    start = jnp.zeros((1, LANES), F32).at[0, :N_EXPERTS].set(pad_start.astype(F32))
    rows = pl.pallas_call(
        _dest_body,
        grid=(n // tm,),
        in_specs=[pl.BlockSpec((tm, LANES), lambda i: (i, 0)), _full((1, LANES))],
        out_specs=pl.BlockSpec((tm, LANES), lambda i: (i, 0)),
        out_shape=jax.ShapeDtypeStruct((n, LANES), F32),
        compiler_params=_params(("parallel",)),
        name="dest",
    )(info, start)
    return rows[:, :2].astype(I32).reshape(-1)


def _dispatch_body(start_ref, end_ref, used_ref, dest_ref, x_ref, free_hbm, xs_hbm, slot_ref,
                   zbuf, sem, zsem):
    step = pl.program_id(0)
    tm = x_ref.shape[0] // ROW_PIECES
    n_blocks = xs_hbm.shape[0] // TILE_ROWS

    @pl.when(step == 0)
    def _():
        mark_free = pltpu.make_async_copy(free_hbm, slot_ref, zsem)
        mark_free.start()
        mark_free.wait()
        zbuf[...] = jnp.zeros(zbuf.shape, F32)

        def fill(block):
            return pltpu.make_async_copy(
                zbuf, xs_hbm.at[pl.ds(pl.multiple_of(block * TILE_ROWS, TILE_ROWS), TILE_ROWS)], zsem)

        def each_fill(action):
            def expert(e, carry):
                @pl.when(end_ref[e] > start_ref[e])
                def _():
                    action(fill(end_ref[e] // MOE_BLOCK - 1))
                return carry

            def tail(blk, carry):
                @pl.when(blk >= used_ref[0])
                def _():
                    action(fill(blk))
                return carry

            lax.fori_loop(0, N_EXPERTS, expert, 0)
            lax.fori_loop(0, n_blocks, tail, 0)

        each_fill(lambda cp: cp.start())
        each_fill(lambda cp: cp.wait())

    copies = []
    for a in range(2 * tm):
        row = dest_ref[a]
        slot_ref[row] = step * (2 * tm) + a
        copies.append(pltpu.make_async_copy(
            x_ref.at[pl.ds((a // 2) * ROW_PIECES, ROW_PIECES)],
            xs_hbm.at[pl.ds(pl.multiple_of(row * ROW_PIECES, ROW_PIECES), ROW_PIECES)], sem))
    _start_all(copies)
    for cp in copies:
        cp.wait()


def _dispatch(xn_tiles, dest, pad_start, pad_end, n_used, n_blocks):
    n = xn_tiles.shape[0] // ROW_PIECES
    tm = min(ROW_TILE, n)
    n_rows = n_blocks * MOE_BLOCK
    grid_spec = pltpu.PrefetchScalarGridSpec(
        num_scalar_prefetch=3,
        grid=(n // tm,),
        in_specs=[pl.BlockSpec((2 * tm,), lambda i, *_: (i,), memory_space=pltpu.SMEM),
                  pl.BlockSpec((tm * ROW_PIECES, LANES), lambda i, *_: (i, 0)),
                  pl.BlockSpec(memory_space=pl.ANY)],
        out_specs=[pl.BlockSpec(memory_space=pl.ANY), pl.BlockSpec(memory_space=pltpu.SMEM)],
        scratch_shapes=[pltpu.VMEM((TILE_ROWS, LANES), F32),
                        pltpu.SemaphoreType.DMA(()), pltpu.SemaphoreType.DMA(())],
    )
    return pl.pallas_call(
        _dispatch_body,
        grid_spec=grid_spec,
        out_shape=[jax.ShapeDtypeStruct((n_rows * ROW_PIECES, LANES), F32),
                   jax.ShapeDtypeStruct((n_rows,), I32)],
        compiler_params=_params(("arbitrary",)),
        name="dispatch",
    )(pad_start, pad_end, n_used, dest, xn_tiles, jnp.full((n_rows,), SLOT_FREE, I32))


def _scatter_copies(idx_ref, buf, hbm, sem):
    return [pltpu.make_async_copy(
        buf.at[pl.ds(r * ROW_PIECES, ROW_PIECES)],
        hbm.at[pl.ds(pl.multiple_of(idx_ref[0, r], ROW_PIECES), ROW_PIECES)], sem)
        for r in range(MOE_BLOCK)]


def _ffn_body(blk_ref, last_ref, sprev_ref, scur_ref, xs_ref, wg_ref, wu_ref, wd_ref, yt_hbm,
              ybuf, wg16, wu16, wd16, ssem):
    b = pl.program_id(0)
    last = last_ref[0]

    @pl.when(b <= last)
    def _():
        slot = b % 2
        other = 1 - slot

        @pl.when(b == 0)
        def _():
            ybuf[...] = jnp.zeros(ybuf.shape, F32)

        @pl.when(b > 0)
        def _():
            for cp in _scatter_copies(sprev_ref, ybuf.at[slot], yt_hbm, ssem.at[slot]):
                cp.wait()

        @pl.when((b == 0) | (blk_ref[b] != blk_ref[jnp.maximum(b - 1, 0)]))
        def _():
            wg16[...] = wg_ref[...].astype(BF16)
            wu16[...] = wu_ref[...].astype(BF16)
            wd16[...] = wd_ref[...].astype(BF16)

        _start_all(_scatter_copies(scur_ref, ybuf.at[other], yt_hbm, ssem.at[other]))
        xb = jnp.concatenate([_load_token_piece(xs_ref, MOE_BLOCK, s).astype(BF16)
                              for s in range(ROW_PIECES)], axis=1)
        g = jnp.dot(xb, wg16[...], preferred_element_type=F32)
        u = jnp.dot(xb, wu16[...], preferred_element_type=F32)
        hb = (_silu(g) * u).astype(BF16)
        _store_token_tiles(ybuf.at[slot], MOE_BLOCK, jnp.dot(hb, wd16[...], preferred_element_type=F32))

        @pl.when(b == last)
        def _():
            for cp in _scatter_copies(scur_ref, ybuf.at[other], yt_hbm, ssem.at[other]):
                cp.wait()


def _ffn(xs, scatter_rows, blk_expert, last_step, n_out_rows, layer, w_gate, w_up, w_down):
    d = D_MODEL
    de = w_gate.shape[-1]
    n_steps = blk_expert.shape[0]
    n_blocks = xs.shape[0] // TILE_ROWS
    idx_spec = lambda shift: pl.BlockSpec(
        (None, 1, MOE_BLOCK), lambda b, e, u: (jnp.maximum(b + shift, 0), 0, 0),
        memory_space=pltpu.SMEM)
    w_spec = lambda rows, cols: pl.BlockSpec((None, None, rows, cols),
                                             lambda b, e, u: (layer, e[b], 0, 0))
    grid_spec = pltpu.PrefetchScalarGridSpec(
        num_scalar_prefetch=2,
        grid=(n_steps,),
        in_specs=[idx_spec(-1), idx_spec(0),
                  pl.BlockSpec((TILE_ROWS, LANES), lambda b, e, u: (jnp.minimum(b, n_blocks - 1), 0)),
                  w_spec(d, de), w_spec(d, de), w_spec(de, d)],
        out_specs=pl.BlockSpec(memory_space=pl.ANY),
        scratch_shapes=[pltpu.VMEM((2, TILE_ROWS, LANES), F32),
                        pltpu.VMEM((d, de), BF16), pltpu.VMEM((d, de), BF16),
                        pltpu.VMEM((de, d), BF16), pltpu.SemaphoreType.DMA((2,))],
    )
    return pl.pallas_call(
        _ffn_body,
        grid_spec=grid_spec,
        out_shape=jax.ShapeDtypeStruct((n_out_rows * ROW_PIECES, LANES), F32),
        compiler_params=_params(("arbitrary",)),
        name="expert_ffn",
    )(blk_expert, last_step, scatter_rows, scatter_rows, xs, w_gate, w_up, w_down)


def _combine_body(x_ref, y0_ref, y1_ref, info_ref, nw_ref, *out_refs):
    tm = x_ref.shape[0]
    info = info_ref[...]
    w0 = info[:, INFO_W0:INFO_W0 + 1]
    w1 = info[:, INFO_W1:INFO_W1 + 1]
    x = jnp.concatenate(
        [x_ref[:, s * LANES:(s + 1) * LANES] + w0 * _load_token_piece(y0_ref, tm, s)
         + w1 * _load_token_piece(y1_ref, tm, s) for s in range(ROW_PIECES)], axis=1)
    xn = x * lax.rsqrt(jnp.mean(x * x, -1, keepdims=True) + EPS) * nw_ref[...]
    if len(out_refs) == 2:
        out_refs[0][...] = x
    out_refs[-1][...] = xn.astype(out_refs[-1].dtype)


def _combine(x2d, yt, info, next_norm_w, final):
    n, d = x2d.shape
    tm = MOE_BLOCK
    row = lambda width: pl.BlockSpec((tm, width), lambda i: (i, 0))
    slot0 = pl.BlockSpec((tm * ROW_PIECES, LANES), lambda i: (i, 0))
    slot1 = pl.BlockSpec((tm * ROW_PIECES, LANES), lambda i: (i + n // tm + 1, 0))
    out_shape = [jax.ShapeDtypeStruct((n, d), F32)]
    if not final:
        out_shape.append(jax.ShapeDtypeStruct((n, d), BF16))
    return pl.pallas_call(
        _combine_body,
        grid=(n // tm,),
        in_specs=[row(d), slot0, slot1, row(LANES), _full((1, d))],
        out_specs=[row(d)] * len(out_shape),
        out_shape=out_shape,
        compiler_params=_params(("parallel",)),
        name="combine",
    )(x2d, yt, yt, info, next_norm_w.reshape(1, d))


def _hier_moe(x2d, norm_w, router_grp, router_exp, layer, w_gate, w_up, w_down, next_norm_w, final):
    n, _ = x2d.shape
    xn_tiles, info, cnt = _router(x2d, norm_w, router_grp, router_exp)
    counts = cnt[0, :N_EXPERTS].astype(I32)
    padded = (counts + MOE_BLOCK - 1) // MOE_BLOCK * MOE_BLOCK
    pad_end = jnp.cumsum(padded)
    pad_start = pad_end - padded
    n_blocks = (2 * n + N_EXPERTS * (MOE_BLOCK - 1)) // MOE_BLOCK + 1
    n_used = (pad_end[-1:] // MOE_BLOCK).astype(I32)
    dest = _dest(info, pad_start)
    xs, slots = _dispatch(xn_tiles, dest, pad_start, pad_end, n_used, n_blocks)
    spare = n + jnp.arange(MOE_BLOCK, dtype=I32)
    scatter_rows = jnp.where(slots < 0, jnp.tile(spare, n_blocks),
                             (slots & 1) * (n + MOE_BLOCK) + (slots >> 1))
    scatter_rows = jnp.concatenate([spare, scatter_rows, spare]) * ROW_PIECES
    blk_start = jnp.arange(n_blocks + 1, dtype=I32) * MOE_BLOCK
    blk_expert = jnp.minimum(jnp.sum((pad_end[None, :] <= blk_start[:, None]).astype(I32), axis=1),
                             N_EXPERTS - 1)
    yt = _ffn(xs, scatter_rows.reshape(n_blocks + 2, 1, MOE_BLOCK), blk_expert, n_used,
              2 * n + MOE_BLOCK, layer, w_gate, w_up, w_down)
    return _combine(x2d, yt, info, next_norm_w, final)


def _mixer_block(x2d, xn2d, bsz, w_in, gdn_conv_w, gdn_a_log, gdn_dt_bias, gdn_norm_w,
                 conf_conv_w, conf_conv_b, conf_ln_w, conf_ln_b, gmlp_ln_w, gmlp_ln_b,
                 gmlp_w_s, gmlp_b_s, sc_conv_w, w_branch, w_out):
    n, d = x2d.shape
    t_len = n // bsz
    xn = xn2d.reshape(bsz, t_len, d)
    c0 = 2 * GDN_QK + 2 * GDN_V
    c1 = c0 + GDN_HEADS
    c2 = c1 + GDN_HEADS
    c3 = c2 + 2 * BRANCH_WIDTH
    c4 = c3 + 2 * BRANCH_WIDTH
    c5 = c4 + 3 * BRANCH_WIDTH
    w16 = w_in.astype(BF16)
    ys = (
        _gdn(xn, w16[:, :c0], w16[:, c0:c1], w16[:, c1:c2], gdn_conv_w, gdn_a_log, gdn_dt_bias,
             gdn_norm_w),
        _conformer(xn, w16[:, c2:c3], conf_conv_w, conf_conv_b, conf_ln_w, conf_ln_b),
        _gmlp(xn, w16[:, c3:c4], gmlp_ln_w, gmlp_ln_b, gmlp_w_s, gmlp_b_s),
        _shortconv(xn, w16[:, c4:c5], sc_conv_w),
    )
    ys = [y.reshape(bsz * t_len, BRANCH_WIDTH) for y in ys]
    return _merge(x2d, xn2d, ys, w16[:, c5:], w_branch.astype(BF16), w_out.astype(BF16))


def kernel(x, mix_norm_w, w_in, gdn_conv_w, gdn_A_log, gdn_dt_bias, gdn_norm_w, conf_conv_w,
           conf_conv_b, conf_ln_w, conf_ln_b, gmlp_ln_w, gmlp_ln_b, gmlp_w_s, gmlp_b_s, sc_conv_w,
           w_branch, w_out, moe_norm_w, router_grp, router_exp, w_gate, w_up, w_down, final_norm_w):
    bsz, t_len, d = x.shape
    depth = mix_norm_w.shape[0]
    x2d = x.reshape(bsz * t_len, d)
    xn2d = _rmsnorm(x2d, mix_norm_w[0], BF16)
    for l in range(depth):
        x2d = _mixer_block(x2d, xn2d, bsz, w_in[l], gdn_conv_w[l], gdn_A_log[l], gdn_dt_bias[l],
                           gdn_norm_w[l], conf_conv_w[l], conf_conv_b[l], conf_ln_w[l],
                           conf_ln_b[l], gmlp_ln_w[l], gmlp_ln_b[l], gmlp_w_s[l], gmlp_b_s[l],
                           sc_conv_w[l], w_branch[l], w_out[l])
        final = l == depth - 1
        next_norm_w = final_norm_w if final else mix_norm_w[l + 1]
        outs = _hier_moe(x2d, moe_norm_w[l], router_grp[l], router_exp[l], l, w_gate, w_up, w_down,
                         next_norm_w, final)
        if final:
            return outs[0].reshape(bsz, t_len, d)
        x2d, xn2d = outs
```

```python
import functools

import jax
import jax.numpy as jnp
from jax import lax
from jax.experimental import pallas as pl
from jax.experimental.pallas import tpu as pltpu

F32 = jnp.float32
BF16 = jnp.bfloat16
I32 = jnp.int32

D_MODEL = 1024
BRANCH_WIDTH = 512
N_BRANCH = 4
GDN_HEADS = 4
GDN_DK = 128
GDN_DV = 128
GDN_CONV = 4
GDN_CHUNK = 64
GDN_QK = GDN_HEADS * GDN_DK
GDN_V = GDN_HEADS * GDN_DV
CONF_KERNEL = 31
GMLP_GROUPS = 4
GMLP_CHUNK = 128
SC_KERNEL = 3
MOE_GROUPS = 8
MOE_PER_GROUP = 8
N_EXPERTS = MOE_GROUPS * MOE_PER_GROUP
D_EXPERT = 512
MOE_BLOCK = 256
EPS = 1e-6

LANES = 128
SUBLANES = 8
VMEM_LIMIT = 56 * 1024 * 1024

SEQ_TILE = 512
ROW_TILE = 512
CONF_HALO = 32
SMALL_HALO = SUBLANES
CONV_ROWS = 256

INFO_E0, INFO_E1, INFO_W0, INFO_W1, INFO_R0, INFO_R1 = range(6)
GRP_LANE0 = N_EXPERTS


def _params(sem):
    return pltpu.CompilerParams(dimension_semantics=sem, vmem_limit_bytes=VMEM_LIMIT)


def _full(shape):
    nd = len(shape)
    return pl.BlockSpec(shape, lambda *_: (0,) * nd)


def _bdot(a, b):
    return jnp.dot(a.astype(BF16), b.astype(BF16), preferred_element_type=F32)


def _sigmoid(x):
    return 1.0 / (1.0 + jnp.exp(-x))


def _silu(x):
    return x * _sigmoid(x)


def _rms_body(x_ref, w_ref, o_ref):
    x = x_ref[...]
    ms = jnp.mean(x * x, -1, keepdims=True)
    o_ref[...] = (x * lax.rsqrt(ms + EPS) * w_ref[...]).astype(o_ref.dtype)


def _rmsnorm(x2d, w, out_dtype):
    n, d = x2d.shape
    tm = min(ROW_TILE, n)
    return pl.pallas_call(
        _rms_body,
        grid=(n // tm,),
        in_specs=[pl.BlockSpec((tm, d), lambda i: (i, 0)), _full((1, d))],
        out_specs=pl.BlockSpec((tm, d), lambda i: (i, 0)),
        out_shape=jax.ShapeDtypeStruct((n, d), out_dtype),
        compiler_params=_params(("parallel",)),
        name="rmsnorm",
    )(x2d, w.reshape(1, d))


def _carry_halo(buf, halo, tile):
    t = pl.program_id(1)

    @pl.when(t == 0)
    def _():
        buf[0:halo, :] = jnp.zeros((halo, buf.shape[1]), buf.dtype)

    @pl.when(t != 0)
    def _():
        buf[0:halo, :] = buf[tile:tile + halo, :]


def _causal_taps(buf, cw_ref, halo, row0, rows, init):
    k_w = cw_ref.shape[0]
    acc = init
    for k in range(k_w):
        off = halo - (k_w - 1) + k + row0
        acc = acc + cw_ref[k:k + 1, :] * buf[off:off + rows, :]
    return acc


def _causal_taps_by_phase(buf, cw_ref, halo, row0, rows, init):
    k_w = cw_ref.shape[0]
    base = halo - (k_w - 1)
    acc = init
    for phase in range(SUBLANES):
        taps = [k for k in range(k_w) if (base + k) % SUBLANES == phase]
        span = rows + (SUBLANES if phase else 0)
        part = None
        for k in taps:
            off = base + k - phase + row0
            term = cw_ref[k:k + 1, :] * buf[off:off + span, :]
            part = term if part is None else part + term
        if part is not None:
            acc = acc + part[phase:phase + rows, :]
    return acc


def _layernorm(x, w, b):
    mu = jnp.mean(x, -1, keepdims=True)
    xc = x - mu
    var = jnp.mean(xc * xc, -1, keepdims=True)
    return xc * lax.rsqrt(var + EPS) * w + b


def _seq_call(body, xn, weights, out_width, scratch, name):
    bsz, t_len, d = xn.shape
    tile = min(SEQ_TILE, t_len)
    in_specs = [pl.BlockSpec((None, tile, d), lambda b, t: (b, t, 0))]
    in_specs += [_full(w.shape) for w in weights]
    return pl.pallas_call(
        body,
        grid=(bsz, t_len // tile),
        in_specs=in_specs,
        out_specs=pl.BlockSpec((None, tile, out_width), lambda b, t: (b, t, 0)),
        out_shape=jax.ShapeDtypeStruct((bsz, t_len, out_width), BF16),
        scratch_shapes=scratch(tile),
        compiler_params=_params(("parallel", "arbitrary")),
        name=name,
    )(xn, *weights)


def _conf_body(xn_ref, w_ref, cw_ref, cb_ref, lw_ref, lb_ref, o_ref, ybuf):
    tile, width = o_ref.shape
    _carry_halo(ybuf, CONF_HALO, tile)
    h = jnp.dot(xn_ref[...], w_ref[...], preferred_element_type=F32)
    ybuf[CONF_HALO:CONF_HALO + tile, :] = h[:, :width] * _sigmoid(h[:, width:])
    for r in range(tile // CONV_ROWS):
        init = jnp.broadcast_to(cb_ref[...], (CONV_ROWS, width))
        y = _causal_taps_by_phase(ybuf, cw_ref, CONF_HALO, r * CONV_ROWS, CONV_ROWS, init)
        y = _layernorm(y, lw_ref[...], lb_ref[...])
        o_ref[r * CONV_ROWS:(r + 1) * CONV_ROWS, :] = _silu(y).astype(o_ref.dtype)


def _conformer(xn, w, conv_w, conv_b, ln_w, ln_b):
    width = BRANCH_WIDTH
    weights = [w, conv_w, conv_b.reshape(1, width), ln_w.reshape(1, width), ln_b.reshape(1, width)]
    scratch = lambda tile: [pltpu.VMEM((CONF_HALO + tile, width), F32)]
    return _seq_call(_conf_body, xn, weights, width, scratch, "conformer")


def _sc_body(xn_ref, w_ref, cw_ref, o_ref, ubuf, bbuf):
    tile, width = o_ref.shape
    _carry_halo(ubuf, SMALL_HALO, tile)
    h = jnp.dot(xn_ref[...], w_ref[...], preferred_element_type=F32)
    bbuf[...] = h[:, :width]
    ubuf[SMALL_HALO:SMALL_HALO + tile, :] = h[:, width:2 * width] * h[:, 2 * width:]
    for r in range(tile // CONV_ROWS):
        rows = slice(r * CONV_ROWS, (r + 1) * CONV_ROWS)
        init = jnp.zeros((CONV_ROWS, width), F32)
        y = _causal_taps(ubuf, cw_ref, SMALL_HALO, r * CONV_ROWS, CONV_ROWS, init)
        o_ref[rows, :] = (bbuf[rows, :] * y).astype(o_ref.dtype)


def _shortconv(xn, w, conv_w):
    width = BRANCH_WIDTH
    scratch = lambda tile: [pltpu.VMEM((SMALL_HALO + tile, width), F32), pltpu.VMEM((tile, width), F32)]
    return _seq_call(_sc_body, xn, [w, conv_w], width, scratch, "shortconv")


def _gmlp_body(xn_ref, w_ref, lw_ref, lb_ref, ws_ref, bs_ref, o_ref, ubuf, vbuf):
    tile, width = o_ref.shape
    gc = width // GMLP_GROUPS
    h = jnp.dot(xn_ref[...], w_ref[...], preferred_element_type=F32)
    g = jax.nn.gelu(h)
    ubuf[...] = g[:, :width]
    vbuf[...] = _layernorm(g[:, width:], lw_ref[...], lb_ref[...]).astype(BF16)
    row = lax.broadcasted_iota(I32, (GMLP_CHUNK, GMLP_CHUNK), 0)
    col = lax.broadcasted_iota(I32, (GMLP_CHUNK, GMLP_CHUNK), 1)
    for gi in range(GMLP_GROUPS):
        w_c = jnp.where(row >= col, ws_ref[gi], 0.0).astype(BF16)
        cols = slice(gi * gc, (gi + 1) * gc)
        for c in range(tile // GMLP_CHUNK):
            rows = slice(c * GMLP_CHUNK, (c + 1) * GMLP_CHUNK)
            mixed = jnp.dot(w_c, vbuf[rows, cols], preferred_element_type=F32) + bs_ref[gi]
            o_ref[rows, cols] = (ubuf[rows, cols] * mixed).astype(o_ref.dtype)


def _gmlp(xn, w, ln_w, ln_b, w_s, b_s):
    width = BRANCH_WIDTH
    gc = width // GMLP_GROUPS
    b_full = jnp.broadcast_to(b_s[:, :, None], (GMLP_GROUPS, GMLP_CHUNK, gc))
    weights = [w, ln_w.reshape(1, width), ln_b.reshape(1, width), w_s, b_full]
    scratch = lambda tile: [pltpu.VMEM((tile, width), F32), pltpu.VMEM((tile, width), BF16)]
    return _seq_call(_gmlp_body, xn, weights, width, scratch, "gmlp")


def _softplus(x):
    return jnp.maximum(x, 0.0) + jnp.log(1.0 + jnp.exp(-jnp.abs(x)))


def _bmm(a, b):
    return jnp.einsum("nij,njk->nik", a.astype(BF16), b.astype(BF16), preferred_element_type=F32)


def _bmm_nt(a, b):
    return jnp.einsum("nid,njd->nij", a.astype(BF16), b.astype(BF16), preferred_element_type=F32)


def _bmm_tn(a, b):
    return jnp.einsum("ncd,nce->nde", a.astype(BF16), b.astype(BF16), preferred_element_type=F32)


def _unit_lower_inverse(strict_lower, eye):
    size = strict_lower.shape[-1]
    inv = eye - strict_lower
    power = strict_lower
    span = 2
    while span < size:
        power = _bmm(power, power)
        inv = inv + _bmm(inv, power)
        span *= 2
    return inv


def _split3(x):
    hi = x.astype(BF16)
    r1 = x - hi.astype(F32)
    mid = r1.astype(BF16)
    lo = (r1 - mid.astype(F32)).astype(BF16)
    return hi, mid, lo


def _gdn_body(xn_ref, w_ref, wb_ref, wa_ref, cw_ref, alog_ref, dtb_ref, nw_ref, o_ref,
              hbuf, qbuf, kbuf, vbuf, zbuf, gbuf, bbuf, s_ref, wc_ref, bc_ref, qp_ref, op_ref, egl_ref):
    tile = o_ref.shape[0]
    chunk = GDN_CHUNK
    t = pl.program_id(1)
    _carry_halo(hbuf, SMALL_HALO, tile)

    @pl.when(t == 0)
    def _():
        s_ref[...] = jnp.zeros(s_ref.shape, F32)

    xn = xn_ref[...]
    h = jnp.dot(xn, w_ref[...], preferred_element_type=F32)
    n_conv = 2 * GDN_QK + GDN_V
    hbuf[SMALL_HALO:SMALL_HALO + tile, :] = h[:, :n_conv]
    zbuf[...] = h[:, n_conv:]
    bbuf[...] = _sigmoid(jnp.dot(xn, wb_ref[...], preferred_element_type=F32))
    a_raw = jnp.dot(xn, wa_ref[...], preferred_element_type=F32)
    gbuf[...] = -jnp.exp(alog_ref[...]) * _softplus(a_raw + dtb_ref[...])

    for r in range(tile // CONV_ROWS):
        rows = slice(r * CONV_ROWS, (r + 1) * CONV_ROWS)
        init = jnp.zeros((CONV_ROWS, n_conv), F32)
        qkv = _silu(_causal_taps(hbuf, cw_ref, SMALL_HALO, r * CONV_ROWS, CONV_ROWS, init))
        for hd in range(GDN_HEADS):
            q = qkv[:, hd * GDN_DK:(hd + 1) * GDN_DK]
            k = qkv[:, GDN_QK + hd * GDN_DK:GDN_QK + (hd + 1) * GDN_DK]
            q = q * lax.rsqrt(jnp.sum(q * q, -1, keepdims=True) + EPS) * (GDN_DK ** -0.5)
            k = k * lax.rsqrt(jnp.sum(k * k, -1, keepdims=True) + EPS)
            qbuf[rows, hd * GDN_DK:(hd + 1) * GDN_DK] = q
            kbuf[rows, hd * GDN_DK:(hd + 1) * GDN_DK] = k
        vbuf[rows, :] = qkv[:, 2 * GDN_QK:]

    n_chunks = tile // chunk
    n_batch = GDN_HEADS * n_chunks
    row = lax.broadcasted_iota(I32, (chunk, chunk), 0)
    col = lax.broadcasted_iota(I32, (chunk, chunk), 1)
    tril = row >= col
    strict = row > col
    eye = jnp.where(row == col, 1.0, 0.0)

    def per_head(buf, width):
        return jnp.concatenate(
            [buf[:, hd * width:(hd + 1) * width].reshape(n_chunks, chunk, width)
             for hd in range(GDN_HEADS)], axis=0)

    g_lo, g_mid, g_hi = reversed(_split3(gbuf[...].reshape(n_chunks, chunk, LANES)))
    ones_tril = jnp.broadcast_to(jnp.where(tril, 1.0, 0.0).astype(BF16), (n_chunks, chunk, chunk))
    g_cum = _bmm(ones_tril, g_lo) + _bmm(ones_tril, g_mid) + _bmm(ones_tril, g_hi)
    g_cum_t = [g_cum[c].T for c in range(n_chunks)]
    g_col = jnp.concatenate([g_cum[:, :, hd:hd + 1] for hd in range(GDN_HEADS)], axis=0)
    g_row = jnp.concatenate(
        [g_cum_t[c][hd:hd + 1, :][None] for hd in range(GDN_HEADS) for c in range(n_chunks)], axis=0)
    b3 = bbuf[...].reshape(n_chunks, chunk, LANES)
    b_col = jnp.concatenate([b3[:, :, hd:hd + 1] for hd in range(GDN_HEADS)], axis=0)
    g_last = g_col[:, chunk - 1:chunk, :]

    q = per_head(qbuf[...], GDN_DK)
    k = per_head(kbuf[...], GDN_DK)
    v = per_head(vbuf[...], GDN_DV)
    decay = jnp.where(tril, jnp.exp(jnp.where(tril, g_col - g_row, 0.0)), 0.0)
    e_g = jnp.exp(g_col)
    k_beta = k * b_col
    inv = _unit_lower_inverse(jnp.where(strict, _bmm_nt(k_beta, k) * decay, 0.0), eye)
    uw = _bmm(inv, jnp.concatenate([v * b_col, k_beta * e_g], axis=2))
    u = uw[:, :, :GDN_DV]
    w = uw[:, :, GDN_DV:]
    attn = _bmm_nt(q, k) * decay
    k_dec = k * jnp.exp(g_last - g_col)
    wc_ref[...] = _bmm_tn(k_dec, w).astype(BF16)
    bc_ref[...] = _bmm_tn(k_dec, u)
    qp_ref[...] = (q * e_g - _bmm(attn, w)).astype(BF16)
    op_ref[...] = _bmm(attn, u)
    egl_ref[...] = jnp.broadcast_to(jnp.exp(g_last), (n_batch, 1, GDN_DV))

    for c in range(n_chunks):
        rows = slice(c * chunk, (c + 1) * chunk)
        for hd in range(GDN_HEADS):
            n = hd * n_chunks + c
            cols = slice(hd * GDN_DV, (hd + 1) * GDN_DV)
            state = s_ref[hd]
            s16 = state.astype(BF16)
            o = jnp.dot(qp_ref[n], s16, preferred_element_type=F32) + op_ref[n]
            s_ref[hd] = (state * egl_ref[n] + bc_ref[n]
                         - jnp.dot(wc_ref[n], s16, preferred_element_type=F32))
            o = o * lax.rsqrt(jnp.mean(o * o, -1, keepdims=True) + EPS) * nw_ref[...]
            o_ref[rows, cols] = (o * _silu(zbuf[rows, cols])).astype(o_ref.dtype)


def _lane_pad(vec):
    return jnp.zeros((1, LANES), F32).at[0, :vec.shape[0]].set(vec)


def _gdn(xn, w, w_beta, w_a, conv_w, a_log, dt_bias, norm_w):
    pad = ((0, 0), (0, LANES - GDN_HEADS))
    weights = [w, jnp.pad(w_beta, pad), jnp.pad(w_a, pad), conv_w,
               _lane_pad(a_log), _lane_pad(dt_bias), norm_w.reshape(1, GDN_DV)]
    n_conv = 2 * GDN_QK + GDN_V

    def scratch(tile):
        n_batch = GDN_HEADS * (tile // GDN_CHUNK)
        return [pltpu.VMEM((SMALL_HALO + tile, n_conv), F32),
                pltpu.VMEM((tile, GDN_QK), F32), pltpu.VMEM((tile, GDN_QK), F32),
                pltpu.VMEM((tile, GDN_V), F32), pltpu.VMEM((tile, GDN_V), F32),
                pltpu.VMEM((tile, LANES), F32), pltpu.VMEM((tile, LANES), F32),
                pltpu.VMEM((GDN_HEADS, GDN_DK, GDN_DV), F32),
                pltpu.VMEM((n_batch, GDN_DK, GDN_DK), BF16),
                pltpu.VMEM((n_batch, GDN_DK, GDN_DV), F32),
                pltpu.VMEM((n_batch, GDN_CHUNK, GDN_DK), BF16),
                pltpu.VMEM((n_batch, GDN_CHUNK, GDN_DV), F32),
                pltpu.VMEM((n_batch, 1, GDN_DV), F32)]

    return _seq_call(_gdn_body, xn, weights, GDN_V, scratch, "gdn")


def _merge_body(x_ref, xn_ref, y0_ref, y1_ref, y2_ref, y3_ref, wg_ref, wb_ref, wo_ref, o_ref):
    xn = xn_ref[...]
    d = xn.shape[1]
    merged = None
    for i, y_ref in enumerate((y0_ref, y1_ref, y2_ref, y3_ref)):
        gate = _sigmoid(jnp.dot(xn, wg_ref[:, i * d:(i + 1) * d], preferred_element_type=F32))
        term = gate * jnp.dot(y_ref[...], wb_ref[i], preferred_element_type=F32)
        merged = term if merged is None else merged + term
    o_ref[...] = x_ref[...] + jnp.dot(merged.astype(BF16), wo_ref[...], preferred_element_type=F32)


def _merge(x2d, xn2d, ys, w_gate, w_branch, w_out):
    n, d = x2d.shape
    tm = min(ROW_TILE, n)
    row = lambda width: pl.BlockSpec((tm, width), lambda i: (i, 0))
    return pl.pallas_call(
        _merge_body,
        grid=(n // tm,),
        in_specs=[row(d), row(d)] + [row(BRANCH_WIDTH)] * N_BRANCH
        + [_full(w_gate.shape), _full(w_branch.shape), _full(w_out.shape)],
        out_specs=row(d),
        out_shape=jax.ShapeDtypeStruct((n, d), F32),
        compiler_params=_params(("parallel",)),
        name="merge",
    )(x2d, xn2d, *ys, w_gate, w_branch, w_out)


def _split2(x):
    hi = x.astype(BF16)
    lo = (x - hi.astype(F32)).astype(BF16)
    return hi, lo


def _router_body(x_ref, nw_ref, wr_ref, xn_ref, info_ref, cnt_ref):
    tm = x_ref.shape[0]

    @pl.when(pl.program_id(0) == 0)
    def _():
        cnt_ref[...] = jnp.zeros(cnt_ref.shape, F32)

    x = x_ref[...]
    xn = x * lax.rsqrt(jnp.mean(x * x, -1, keepdims=True) + EPS) * nw_ref[...]
    _store_token_tiles(xn_ref, tm, xn)
    x_hi, x_lo = _split2(xn)
    logits = (jnp.dot(x_lo, wr_ref[0], preferred_element_type=F32)
              + jnp.dot(x_hi, wr_ref[1], preferred_element_type=F32)
              + jnp.dot(x_hi, wr_ref[0], preferred_element_type=F32))

    lane = lax.broadcasted_iota(I32, (tm, LANES), 1)
    neg = jnp.float32(-jnp.inf)
    big = jnp.int32(2 * LANES)
    is_grp = (lane >= GRP_LANE0) & (lane < GRP_LANE0 + MOE_GROUPS)
    lg = jnp.where(is_grp, logits, neg)
    mg = jnp.max(lg, -1, keepdims=True)
    grp = jnp.min(jnp.where(lg == mg, lane, big), -1, keepdims=True) - GRP_LANE0
    p_sel = 1.0 / jnp.sum(jnp.where(is_grp, jnp.exp(lg - mg), 0.0), -1, keepdims=True)

    in_grp = (lane < N_EXPERTS) & ((lane // MOE_PER_GROUP) == grp)
    le = jnp.where(in_grp, logits, neg)
    m0 = jnp.max(le, -1, keepdims=True)
    e0 = jnp.min(jnp.where(le == m0, lane, big), -1, keepdims=True)
    le1 = jnp.where(lane == e0, neg, le)
    m1 = jnp.max(le1, -1, keepdims=True)
    e1 = jnp.min(jnp.where(le1 == m1, lane, big), -1, keepdims=True)
    r = jnp.exp(m1 - m0)
    w0 = p_sel / (1.0 + r)
    w1 = p_sel * r / (1.0 + r)

    hot0 = lane == e0
    hot1 = lane == e1
    onehot = jnp.where(hot0 | hot1, 1.0, 0.0)
    ti = lax.broadcasted_iota(I32, (tm, tm), 0)
    tj = lax.broadcasted_iota(I32, (tm, tm), 1)
    before = jnp.where(ti > tj, 1.0, 0.0).astype(BF16)
    base = jnp.dot(before, onehot.astype(BF16), preferred_element_type=F32) + cnt_ref[...]
    r0 = jnp.sum(jnp.where(hot0, base, 0.0), -1, keepdims=True)
    r1 = jnp.sum(jnp.where(hot1, base, 0.0), -1, keepdims=True)
    cnt_ref[...] = cnt_ref[...] + jnp.sum(onehot, 0, keepdims=True)

    info = jnp.zeros((tm, LANES), F32)
    for idx, val in ((INFO_E0, e0.astype(F32)), (INFO_E1, e1.astype(F32)), (INFO_W0, w0),
                     (INFO_W1, w1), (INFO_R0, r0), (INFO_R1, r1)):
        info = jnp.where(lane == idx, val, info)
    info_ref[...] = info


def _router(x2d, norm_w, router_grp, router_exp):
    n, d = x2d.shape
    tm = min(ROW_TILE, n)
    w_r = jnp.zeros((d, LANES), F32).at[:, :N_EXPERTS].set(router_exp)
    w_r = w_r.at[:, GRP_LANE0:GRP_LANE0 + MOE_GROUPS].set(router_grp)
    w_parts = jnp.stack(_split2(w_r))
    return pl.pallas_call(
        _router_body,
        grid=(n // tm,),
        in_specs=[pl.BlockSpec((tm, d), lambda i: (i, 0)), _full((1, d)), _full(w_parts.shape)],
        out_specs=[pl.BlockSpec((tm * ROW_PIECES, LANES), lambda i: (i, 0)),
                   pl.BlockSpec((tm, LANES), lambda i: (i, 0)),
                   _full((1, LANES))],
        out_shape=[jax.ShapeDtypeStruct((n * ROW_PIECES, LANES), F32),
                   jax.ShapeDtypeStruct((n, LANES), F32),
                   jax.ShapeDtypeStruct((1, LANES), F32)],
        compiler_params=_params(("arbitrary",)),
        name="router",
    )(x2d, norm_w.reshape(1, d), w_parts)


ROW_PIECES = D_MODEL // LANES
assert ROW_PIECES == SUBLANES


def _store_token_tiles(ref, rows, value):
    for s in range(ROW_PIECES):
        ref[pl.ds(s, rows, stride=ROW_PIECES), :] = value[:, s * LANES:(s + 1) * LANES]


def _load_token_piece(ref, rows, s):
    return ref[pl.ds(s, rows, stride=ROW_PIECES), :]


SLOT_FREE = -1
DMA_THREADS = 2
TILE_ROWS = MOE_BLOCK * ROW_PIECES


def _start_all(copies):
    for r, cp in enumerate(copies):
        cp.start(priority=r % DMA_THREADS)


def _dest_body(info_ref, start_ref, o_ref):
    info = info_ref[...]
    lane = lax.broadcasted_iota(I32, info.shape, 1)
    lane_f = lane.astype(F32)
    start = start_ref[...]
    out = jnp.zeros(info.shape, F32)
    for slot, (e_lane, r_lane) in enumerate(((INFO_E0, INFO_R0), (INFO_E1, INFO_R1))):
        hit = lane_f == info[:, e_lane:e_lane + 1]
        row = jnp.sum(jnp.where(hit, start, 0.0), -1, keepdims=True) + info[:, r_lane:r_lane + 1]
        out = jnp.where(lane == slot, row, out)
    o_ref[...] = out


def _dest(info, pad_start):
    n = info.shape[0]
    tm = min(4 * ROW_TILE, n)
    start = jnp.zeros((1, LANES), F32).at[0, :N_EXPERTS].set(pad_start.astype(F32))
    rows = pl.pallas_call(
        _dest_body,
        grid=(n // tm,),
        in_specs=[pl.BlockSpec((tm, LANES), lambda i: (i, 0)), _full((1, LANES))],
        out_specs=pl.BlockSpec((tm, LANES), lambda i: (i, 0)),
        out_shape=jax.ShapeDtypeStruct((n, LANES), F32),
        compiler_params=_params(("parallel",)),
        name="dest",
    )(info, start)
    return rows[:, :2].astype(I32).reshape(-1)


def _dispatch_body(start_ref, end_ref, used_ref, dest_ref, x_ref, free_hbm, xs_hbm, slot_ref,
                   zbuf, sem, zsem):
    step = pl.program_id(0)
    tm = x_ref.shape[0] // ROW_PIECES
    n_blocks = xs_hbm.shape[0] // TILE_ROWS

    @pl.when(step == 0)
    def _():
        mark_free = pltpu.make_async_copy(free_hbm, slot_ref, zsem)
        mark_free.start()
        mark_free.wait()
        zbuf[...] = jnp.zeros(zbuf.shape, F32)

        def fill(block):
            return pltpu.make_async_copy(
                zbuf, xs_hbm.at[pl.ds(pl.multiple_of(block * TILE_ROWS, TILE_ROWS), TILE_ROWS)], zsem)

        def each_fill(action):
            def expert(e, carry):
                @pl.when(end_ref[e] > start_ref[e])
                def _():
                    action(fill(end_ref[e] // MOE_BLOCK - 1))
                return carry

            def tail(blk, carry):
                @pl.when(blk >= used_ref[0])
                def _():
                    action(fill(blk))
                return carry

            lax.fori_loop(0, N_EXPERTS, expert, 0)
            lax.fori_loop(0, n_blocks, tail, 0)

        each_fill(lambda cp: cp.start())
        each_fill(lambda cp: cp.wait())

    copies = []
    for a in range(2 * tm):
        row = dest_ref[a]
        slot_ref[row] = step * (2 * tm) + a
        copies.append(pltpu.make_async_copy(
            x_ref.at[pl.ds((a // 2) * ROW_PIECES, ROW_PIECES)],
            xs_hbm.at[pl.ds(pl.multiple_of(row * ROW_PIECES, ROW_PIECES), ROW_PIECES)], sem))
    _start_all(copies)
    for cp in copies:
        cp.wait()


def _dispatch(xn_tiles, dest, pad_start, pad_end, n_used, n_blocks):
    n = xn_tiles.shape[0] // ROW_PIECES
    tm = min(ROW_TILE, n)
    n_rows = n_blocks * MOE_BLOCK
    grid_spec = pltpu.PrefetchScalarGridSpec(
        num_scalar_prefetch=3,
        grid=(n // tm,),
        in_specs=[pl.BlockSpec((2 * tm,), lambda i, *_: (i,), memory_space=pltpu.SMEM),
                  pl.BlockSpec((tm * ROW_PIECES, LANES), lambda i, *_: (i, 0)),
                  pl.BlockSpec(memory_space=pl.ANY)],
        out_specs=[pl.BlockSpec(memory_space=pl.ANY), pl.BlockSpec(memory_space=pltpu.SMEM)],
        scratch_shapes=[pltpu.VMEM((TILE_ROWS, LANES), F32),
                        pltpu.SemaphoreType.DMA(()), pltpu.SemaphoreType.DMA(())],
    )
    return pl.pallas_call(
        _dispatch_body,
        grid_spec=grid_spec,
        out_shape=[jax.ShapeDtypeStruct((n_rows * ROW_PIECES, LANES), F32),
                   jax.ShapeDtypeStruct((n_rows,), I32)],
        compiler_params=_params(("arbitrary",)),
        name="dispatch",
    )(pad_start, pad_end, n_used, dest, xn_tiles, jnp.full((n_rows,), SLOT_FREE, I32))


def _scatter_copies(idx_ref, buf, hbm, sem):
    return [pltpu.make_async_copy(
        buf.at[pl.ds(r * ROW_PIECES, ROW_PIECES)],
        hbm.at[pl.ds(pl.multiple_of(idx_ref[0, r], ROW_PIECES), ROW_PIECES)], sem)
        for r in range(MOE_BLOCK)]


def _ffn_body(blk_ref, last_ref, sprev_ref, scur_ref, xs_ref, wg_ref, wu_ref, wd_ref, yt_hbm,
              ybuf, wg16, wu16, wd16, ssem):
    b = pl.program_id(0)
    last = last_ref[0]

    @pl.when(b <= last)
    def _():
        slot = b % 2
        other = 1 - slot

        @pl.when(b == 0)
        def _():
            ybuf[...] = jnp.zeros(ybuf.shape, F32)

        @pl.when(b > 0)
        def _():
            for cp in _scatter_copies(sprev_ref, ybuf.at[slot], yt_hbm, ssem.at[slot]):
                cp.wait()

        @pl.when((b == 0) | (blk_ref[b] != blk_ref[jnp.maximum(b - 1, 0)]))
        def _():
            wg16[...] = wg_ref[...].astype(BF16)
            wu16[...] = wu_ref[...].astype(BF16)
            wd16[...] = wd_ref[...].astype(BF16)

        _start_all(_scatter_copies(scur_ref, ybuf.at[other], yt_hbm, ssem.at[other]))
        xb = jnp.concatenate([_load_token_piece(xs_ref, MOE_BLOCK, s).astype(BF16)
                              for s in range(ROW_PIECES)], axis=1)
        g = jnp.dot(xb, wg16[...], preferred_element_type=F32)
        u = jnp.dot(xb, wu16[...], preferred_element_type=F32)
        hb = (_silu(g) * u).astype(BF16)
        _store_token_tiles(ybuf.at[slot], MOE_BLOCK, jnp.dot(hb, wd16[...], preferred_element_type=F32))

        @pl.when(b == last)
        def _():
            for cp in _scatter_copies(scur_ref, ybuf.at[other], yt_hbm, ssem.at[other]):
                cp.wait()


def _ffn(xs, scatter_rows, blk_expert, last_step, n_out_rows, layer, w_gate, w_up, w_down):
    d = D_MODEL
    de = w_gate.shape[-1]
    n_steps = blk_expert.shape[0]
    n_blocks = xs.shape[0] // TILE_ROWS
    idx_spec = lambda shift: pl.BlockSpec(
        (None, 1, MOE_BLOCK), lambda b, e, u: (jnp.maximum(b + shift, 0), 0, 0),
        memory_space=pltpu.SMEM)
    w_spec = lambda rows, cols: pl.BlockSpec((None, None, rows, cols),
                                             lambda b, e, u: (layer, e[b], 0, 0))
    grid_spec = pltpu.PrefetchScalarGridSpec(
        num_scalar_prefetch=2,
        grid=(n_steps,),
        in_specs=[idx_spec(-1), idx_spec(0),
                  pl.BlockSpec((TILE_ROWS, LANES), lambda b, e, u: (jnp.minimum(b, n_blocks - 1), 0)),
                  w_spec(d, de), w_spec(d, de), w_spec(de, d)],
        out_specs=pl.BlockSpec(memory_space=pl.ANY),
        scratch_shapes=[pltpu.VMEM((2, TILE_ROWS, LANES), F32),
                        pltpu.VMEM((d, de), BF16), pltpu.VMEM((d, de), BF16),
                        pltpu.VMEM((de, d), BF16), pltpu.SemaphoreType.DMA((2,))],
    )
    return pl.pallas_call(
        _ffn_body,
        grid_spec=grid_spec,
        out_shape=jax.ShapeDtypeStruct((n_out_rows * ROW_PIECES, LANES), F32),
        compiler_params=_params(("arbitrary",)),
        name="expert_ffn",
    )(blk_expert, last_step, scatter_rows, scatter_rows, xs, w_gate, w_up, w_down)


def _combine_body(x_ref, y0_ref, y1_ref, info_ref, nw_ref, *out_refs):
    tm = x_ref.shape[0]
    info = info_ref[...]
    w0 = info[:, INFO_W0:INFO_W0 + 1]
    w1 = info[:, INFO_W1:INFO_W1 + 1]
    x = jnp.concatenate(
        [x_ref[:, s * LANES:(s + 1) * LANES] + w0 * _load_token_piece(y0_ref, tm, s)
         + w1 * _load_token_piece(y1_ref, tm, s) for s in range(ROW_PIECES)], axis=1)
    xn = x * lax.rsqrt(jnp.mean(x * x, -1, keepdims=True) + EPS) * nw_ref[...]
    if len(out_refs) == 2:
        out_refs[0][...] = x
    out_refs[-1][...] = xn.astype(out_refs[-1].dtype)


def _combine(x2d, yt, info, next_norm_w, final):
    n, d = x2d.shape
    tm = MOE_BLOCK
    row = lambda width: pl.BlockSpec((tm, width), lambda i: (i, 0))
    slot0 = pl.BlockSpec((tm * ROW_PIECES, LANES), lambda i: (i, 0))
    slot1 = pl.BlockSpec((tm * ROW_PIECES, LANES), lambda i: (i + n // tm + 1, 0))
    out_shape = [jax.ShapeDtypeStruct((n, d), F32)]
    if not final:
        out_shape.append(jax.ShapeDtypeStruct((n, d), BF16))
    return pl.pallas_call(
        _combine_body,
        grid=(n // tm,),
        in_specs=[row(d), slot0, slot1, row(LANES), _full((1, d))],
        out_specs=[row(d)] * len(out_shape),
        out_shape=out_shape,
        compiler_params=_params(("parallel",)),
        name="combine",
    )(x2d, yt, yt, info, next_norm_w.reshape(1, d))


def _hier_moe(x2d, norm_w, router_grp, router_exp, layer, w_gate, w_up, w_down, next_norm_w, final):
    n, _ = x2d.shape
    xn_tiles, info, cnt = _router(x2d, norm_w, router_grp, router_exp)
    counts = cnt[0, :N_EXPERTS].astype(I32)
    padded = (counts + MOE_BLOCK - 1) // MOE_BLOCK * MOE_BLOCK
    pad_end = jnp.cumsum(padded)
    pad_start = pad_end - padded
    n_blocks = (2 * n + N_EXPERTS * (MOE_BLOCK - 1)) // MOE_BLOCK + 1
    n_used = (pad_end[-1:] // MOE_BLOCK).astype(I32)
    dest = _dest(info, pad_start)
    xs, slots = _dispatch(xn_tiles, dest, pad_start, pad_end, n_used, n_blocks)
    spare = n + jnp.arange(MOE_BLOCK, dtype=I32)
    scatter_rows = jnp.where(slots < 0, jnp.tile(spare, n_blocks),
                             (slots & 1) * (n + MOE_BLOCK) + (slots >> 1))
    scatter_rows = jnp.concatenate([spare, scatter_rows, spare]) * ROW_PIECES
    blk_start = jnp.arange(n_blocks + 1, dtype=I32) * MOE_BLOCK
    blk_expert = jnp.minimum(jnp.sum((pad_end[None, :] <= blk_start[:, None]).astype(I32), axis=1),
                             N_EXPERTS - 1)
    yt = _ffn(xs, scatter_rows.reshape(n_blocks + 2, 1, MOE_BLOCK), blk_expert, n_used,
              2 * n + MOE_BLOCK, layer, w_gate, w_up, w_down)
    return _combine(x2d, yt, info, next_norm_w, final)


def _mixer_block(x2d, xn2d, bsz, w_in, gdn_conv_w, gdn_a_log, gdn_dt_bias, gdn_norm_w,
                 conf_conv_w, conf_conv_b, conf_ln_w, conf_ln_b, gmlp_ln_w, gmlp_ln_b,
                 gmlp_w_s, gmlp_b_s, sc_conv_w, w_branch, w_out):
    n, d = x2d.shape
    t_len = n // bsz
    xn = xn2d.reshape(bsz, t_len, d)
    c0 = 2 * GDN_QK + 2 * GDN_V
    c1 = c0 + GDN_HEADS
    c2 = c1 + GDN_HEADS
    c3 = c2 + 2 * BRANCH_WIDTH
    c4 = c3 + 2 * BRANCH_WIDTH
    c5 = c4 + 3 * BRANCH_WIDTH
    w16 = w_in.astype(BF16)
    ys = (
        _gdn(xn, w16[:, :c0], w16[:, c0:c1], w16[:, c1:c2], gdn_conv_w, gdn_a_log, gdn_dt_bias,
             gdn_norm_w),
        _conformer(xn, w16[:, c2:c3], conf_conv_w, conf_conv_b, conf_ln_w, conf_ln_b),
        _gmlp(xn, w16[:, c3:c4], gmlp_ln_w, gmlp_ln_b, gmlp_w_s, gmlp_b_s),
        _shortconv(xn, w16[:, c4:c5], sc_conv_w),
    )
    ys = [y.reshape(bsz * t_len, BRANCH_WIDTH) for y in ys]
    return _merge(x2d, xn2d, ys, w16[:, c5:], w_branch.astype(BF16), w_out.astype(BF16))


def kernel(x, mix_norm_w, w_in, gdn_conv_w, gdn_A_log, gdn_dt_bias, gdn_norm_w, conf_conv_w,
           conf_conv_b, conf_ln_w, conf_ln_b, gmlp_ln_w, gmlp_ln_b, gmlp_w_s, gmlp_b_s, sc_conv_w,
           w_branch, w_out, moe_norm_w, router_grp, router_exp, w_gate, w_up, w_down, final_norm_w):
    bsz, t_len, d = x.shape
    depth = mix_norm_w.shape[0]
    x2d = x.reshape(bsz * t_len, d)
    xn2d = _rmsnorm(x2d, mix_norm_w[0], BF16)
    for l in range(depth):
        x2d = _mixer_block(x2d, xn2d, bsz, w_in[l], gdn_conv_w[l], gdn_A_log[l], gdn_dt_bias[l],
                           gdn_norm_w[l], conf_conv_w[l], conf_conv_b[l], conf_ln_w[l],
                           conf_ln_b[l], gmlp_ln_w[l], gmlp_ln_b[l], gmlp_w_s[l], gmlp_b_s[l],
                           sc_conv_w[l], w_branch[l], w_out[l])
        final = l == depth - 1
        next_norm_w = final_norm_w if final else mix_norm_w[l + 1]
        outs = _hier_moe(x2d, moe_norm_w[l], router_grp[l], router_exp[l], l, w_gate, w_up, w_down,
                         next_norm_w, final)
        if final:
            return outs[0].reshape(bsz, t_len, d)
        x2d, xn2d = outs
```
